```python
import math
import jax
import jax.numpy as jnp
from jax import lax
import numpy as np

D_MODEL = 1024
BATCH = 4
SEQ = 4096
DEPTH = 4

CHUNK = 64
EPS = 1e-6
SGU_BLOCK = 128
SGU_GROUPS = 8
SGU_WIDTH = D_MODEL // 2
SGU_GDIM = SGU_WIDTH // SGU_GROUPS
GDN_HEADS = 4
GDN_DK = 128
GDN_DV = 128
GDN_CONV = 4
GDN_QK = GDN_HEADS * GDN_DK
GDN_V = GDN_HEADS * GDN_DV
CONV_CH = 2 * GDN_QK + GDN_V
IN_EVEN = 2 * SGU_WIDTH + CONV_CH + GDN_V + 2 * GDN_HEADS
SPLITS_EVEN = (SGU_WIDTH, 2 * SGU_WIDTH, 2 * SGU_WIDTH + CONV_CH,
               2 * SGU_WIDTH + CONV_CH + GDN_V, 2 * SGU_WIDTH + CONV_CH + GDN_V + GDN_HEADS)
MIX_EVEN = SGU_WIDTH + GDN_V
SB_HEADS = 16
SB_DIM = D_MODEL // SB_HEADS
SB_BLOCK = 128
D_FF = 2816
N_EXPERTS = 8
TOP_K = 2
N_EVEN = (DEPTH + 1) // 2
N_ODD = DEPTH // 2

kernel_name = 'hybrid_sgu_gdn_stickbreak_moe_trunk'


def rmsnorm(x, g):
    xf = x.astype(jnp.float32)
    y = xf * lax.rsqrt(jnp.mean(xf * xf, axis=-1, keepdims=True) + EPS)
    return (y * g.astype(jnp.float32)).astype(x.dtype)


def layernorm(x, g, b):
    xf = x.astype(jnp.float32)
    mu = jnp.mean(xf, axis=-1, keepdims=True)
    var = jnp.mean(jnp.square(xf - mu), axis=-1, keepdims=True)
    y = (xf - mu) * lax.rsqrt(var + EPS)
    return (y * g.astype(jnp.float32) + b.astype(jnp.float32)).astype(x.dtype)


def l2norm(x):
    xf = x.astype(jnp.float32)
    return xf * lax.rsqrt(jnp.sum(xf * xf, axis=-1, keepdims=True) + EPS)


def swiglu(x, w1, w3, w2):
    return (jax.nn.silu(x @ w1) * (x @ w3)) @ w2


def causal_conv(x, w):
    K = w.shape[0]
    S = x.shape[1]
    xp = jnp.pad(x, ((0, 0), (K - 1, 0), (0, 0)))
    return sum(xp[:, i:i + S] * w[i] for i in range(K))


def spatial_gating(u_pre, v_pre, ln_g, ln_b, w_s, b_s):
    B, S, _ = u_pre.shape
    u = jax.nn.gelu(u_pre)
    v = layernorm(jax.nn.gelu(v_pre), ln_g, ln_b)
    nb = S // SGU_BLOCK
    v = v.reshape(B, nb, SGU_BLOCK, SGU_GROUPS, SGU_GDIM)
    pos_chunk = jnp.arange(SGU_BLOCK) // CHUNK
    mask = pos_chunk[None, :] <= pos_chunk[:, None]
    w = jnp.where(mask[None], w_s, 0.0).astype(v.dtype)
    mixed = jnp.einsum('gts,bnsgc->bntgc', w, v) + b_s.T[None, None, :, :, None].astype(v.dtype)
    return u * mixed.reshape(B, S, SGU_WIDTH)


def chunk_gated_delta(q, k, v, beta, g):
    B, S, H, dk = q.shape
    dv = v.shape[-1]
    N, C = S // CHUNK, CHUNK

    def blk(t):
        return jnp.moveaxis(t.reshape((B, N, C, H) + t.shape[3:]), 3, 1)

    q, k, v, beta, g = blk(q), blk(k), blk(v), blk(beta), blk(g)
    gc = jnp.cumsum(g, axis=-1)
    idx = jnp.arange(C)
    causal = idx[:, None] >= idx[None, :]
    strict = idx[:, None] > idx[None, :]
    gamma = jnp.exp(jnp.where(causal, gc[..., :, None] - gc[..., None, :], -jnp.inf))
    kb = k * beta[..., None]
    L = jnp.where(strict, jnp.einsum('bhnid,bhnjd->bhnij', kb, k) * gamma, 0.0)
    rhs = jnp.concatenate([v * beta[..., None], kb * jnp.exp(gc)[..., None]], axis=-1)
    sol = lax.linalg.triangular_solve(jnp.eye(C, dtype=L.dtype) + L, rhs,
                                      left_side=True, lower=True, unit_diagonal=True)
    U, W = sol[..., :dv], sol[..., dv:]
    attn = jnp.einsum('bhnid,bhnjd->bhnij', q, k) * gamma
    q_dec = q * jnp.exp(gc)[..., None]
    g_last = gc[..., -1]
    k_dec = k * jnp.exp(g_last[..., None] - gc)[..., None]

    def step(state, xs):
        qd, kd, u, w, a, gl = xs
        v_new = u - jnp.einsum('bhcd,bhde->bhce', w, state)
        o = jnp.einsum('bhcd,bhde->bhce', qd, state) + jnp.einsum('bhij,bhje->bhie', a, v_new)
        state = state * jnp.exp(gl)[..., None, None] + jnp.einsum('bhcd,bhce->bhde', kd, v_new)
        return state, o

    xs = tuple(jnp.moveaxis(t, 2, 0) for t in (q_dec, k_dec, U, W, attn, g_last))
    state0 = jnp.zeros((B, H, dk, dv), jnp.float32)
    _, o = lax.scan(step, state0, xs)
    o = jnp.moveaxis(o, 0, 2)
    return o.transpose(0, 2, 3, 1, 4).reshape(B, S, H, dv)


def gated_deltanet(qkv_pre, gate_pre, b_pre, a_pre, conv_w, a_log, dt_bias, o_norm):
    B, S, _ = qkv_pre.shape
    qkv = jax.nn.silu(causal_conv(qkv_pre, conv_w))
    q, k, v = jnp.split(qkv, [GDN_QK, 2 * GDN_QK], axis=-1)
    q = l2norm(q.reshape(B, S, GDN_HEADS, GDN_DK)) * (GDN_DK ** -0.5)
    k = l2norm(k.reshape(B, S, GDN_HEADS, GDN_DK))
    v = v.reshape(B, S, GDN_HEADS, GDN_DV).astype(jnp.float32)
    beta = jax.nn.sigmoid(b_pre.astype(jnp.float32))
    g = -jnp.exp(a_log.astype(jnp.float32)) * jax.nn.softplus(a_pre.astype(jnp.float32) + dt_bias.astype(jnp.float32))
    o = chunk_gated_delta(q, k, v, beta, g)
    o = rmsnorm(o, o_norm) * jax.nn.silu(gate_pre.astype(jnp.float32).reshape(B, S, GDN_HEADS, GDN_DV))
    return o.reshape(B, S, GDN_V).astype(qkv_pre.dtype)


def even_mixer(xn, w_in, sgu_ln_g, sgu_ln_b, sgu_w, sgu_b, conv_w, a_log, dt_bias, o_norm, w_out):
    z = xn @ w_in
    u_pre, v_pre, qkv_pre, gate_pre, b_pre, a_pre = jnp.split(z, SPLITS_EVEN, axis=-1)
    y_a = spatial_gating(u_pre, v_pre, sgu_ln_g, sgu_ln_b, sgu_w, sgu_b)
    y_b = gated_deltanet(qkv_pre, gate_pre, b_pre, a_pre, conv_w, a_log, dt_bias, o_norm)
    return jnp.concatenate([y_a, y_b], axis=-1) @ w_out


def stick_breaking_mixer(xn, w_qkv, w_out):
    B, S, _ = xn.shape
    q, k, v = jnp.split(xn @ w_qkv, 3, axis=-1)

    def heads(t):
        return t.reshape(B, S, SB_HEADS, SB_DIM).transpose(0, 2, 1, 3)

    q, k, v = heads(q), heads(k), heads(v)
    scale = SB_DIM ** -0.5
    outs = []
    for qb in range(S // SB_BLOCK):
        kn = (qb + 1) * SB_BLOCK
        qi = q[:, :, qb * SB_BLOCK:kn]
        z = jnp.einsum('bhtd,bhsd->bhts', qi, k[:, :, :kn]).astype(jnp.float32) * scale
        t_pos = qb * SB_BLOCK + jnp.arange(SB_BLOCK)
        s_pos = jnp.arange(kn)
        mask = s_pos[None, :] < t_pos[:, None]
        log_keep = jnp.where(mask, jax.nn.log_sigmoid(-z), 0.0)
        after = lax.cumsum(log_keep, axis=3, reverse=True) - log_keep
        att = jnp.where(mask, jnp.exp(jax.nn.log_sigmoid(z) + after), 0.0)
        outs.append(jnp.einsum('bhts,bhsd->bhtd', att.astype(v.dtype), v[:, :, :kn]))
    o = jnp.concatenate(outs, axis=2).transpose(0, 2, 1, 3).reshape(B, S, D_MODEL)
    return o @ w_out


def moe_swiglu(xn, w_router, w1, w3, w2):
    B, S, D = xn.shape
    xf = xn.reshape(-1, D)
    logits = (xf @ w_router).astype(jnp.float32)
    vals, idx = lax.top_k(logits, TOP_K)
    gates = jax.nn.softmax(vals, axis=-1)
    combine = jnp.sum(jax.nn.one_hot(idx, N_EXPERTS, dtype=jnp.float32) * gates[..., None], axis=1)
    y = jnp.zeros_like(xf)
    for e in range(N_EXPERTS):
        y = y + combine[:, e:e + 1].astype(xf.dtype) * swiglu(xf, w1[e], w3[e], w2[e])
    return y.reshape(B, S, D)


def setup_inputs(seed: int = 0) -> dict:
    key = jax.random.key(seed)
    ks = iter(jax.random.split(key, 32))
    D = D_MODEL

    def nrm(shape, scale):
        return scale * jax.random.normal(next(ks), shape, jnp.float32)

    def gain(shape):
        return 1.0 + 0.05 * jax.random.normal(next(ks), shape, jnp.float32)

    x = nrm((BATCH, SEQ, D), 1.0)
    e_norm1 = gain((N_EVEN, D))
    e_w_in = nrm((N_EVEN, D, IN_EVEN), D ** -0.5)
    e_sgu_ln_g = gain((N_EVEN, SGU_WIDTH))
    e_sgu_ln_b = nrm((N_EVEN, SGU_WIDTH), 0.02)
    e_sgu_w = nrm((N_EVEN, SGU_GROUPS, SGU_BLOCK, SGU_BLOCK), SGU_BLOCK ** -0.5)
    e_sgu_b = gain((N_EVEN, SGU_GROUPS, SGU_BLOCK))
    e_conv_w = nrm((N_EVEN, GDN_CONV, CONV_CH), GDN_CONV ** -0.5)
    e_a_log = jnp.log(jax.random.uniform(next(ks), (N_EVEN, GDN_HEADS), jnp.float32, 1.0, 16.0))
    dt = jnp.exp(jax.random.uniform(next(ks), (N_EVEN, GDN_HEADS), jnp.float32,
                                    math.log(1e-3), math.log(1e-1)))
    e_dt_bias = dt + jnp.log(-jnp.expm1(-dt))
    e_o_norm = gain((N_EVEN, GDN_DV))
    e_w_out = nrm((N_EVEN, MIX_EVEN, D), MIX_EVEN ** -0.5)
    e_norm2 = gain((N_EVEN, D))
    e_ffn_w1 = nrm((N_EVEN, D, D_FF), D ** -0.5)
    e_ffn_w3 = nrm((N_EVEN, D, D_FF), D ** -0.5)
    e_ffn_w2 = nrm((N_EVEN, D_FF, D), D_FF ** -0.5)
    o_norm1 = gain((N_ODD, D))
    o_w_qkv = nrm((N_ODD, D, 3 * D), D ** -0.5)
    o_w_out = nrm((N_ODD, D, D), D ** -0.5)
    o_norm2 = gain((N_ODD, D))
    o_router = nrm((N_ODD, D, N_EXPERTS), D ** -0.5)
    o_moe_w1 = nrm((N_ODD, N_EXPERTS, D, D_FF), D ** -0.5)
    o_moe_w3 = nrm((N_ODD, N_EXPERTS, D, D_FF), D ** -0.5)
    o_moe_w2 = nrm((N_ODD, N_EXPERTS, D_FF, D), D_FF ** -0.5)
    final_norm = gain((D,))
    return {'x': x, 'e_norm1': e_norm1, 'e_w_in': e_w_in, 'e_sgu_ln_g': e_sgu_ln_g,
            'e_sgu_ln_b': e_sgu_ln_b, 'e_sgu_w': e_sgu_w, 'e_sgu_b': e_sgu_b, 'e_conv_w': e_conv_w,
            'e_a_log': e_a_log, 'e_dt_bias': e_dt_bias, 'e_o_norm': e_o_norm, 'e_w_out': e_w_out,
            'e_norm2': e_norm2, 'e_ffn_w1': e_ffn_w1, 'e_ffn_w3': e_ffn_w3, 'e_ffn_w2': e_ffn_w2,
            'o_norm1': o_norm1, 'o_w_qkv': o_w_qkv, 'o_w_out': o_w_out, 'o_norm2': o_norm2,
            'o_router': o_router, 'o_moe_w1': o_moe_w1, 'o_moe_w3': o_moe_w3, 'o_moe_w2': o_moe_w2,
            'final_norm': final_norm}


def reference(x, e_norm1, e_w_in, e_sgu_ln_g, e_sgu_ln_b, e_sgu_w, e_sgu_b, e_conv_w, e_a_log,
              e_dt_bias, e_o_norm, e_w_out, e_norm2, e_ffn_w1, e_ffn_w3, e_ffn_w2, o_norm1, o_w_qkv,
              o_w_out, o_norm2, o_router, o_moe_w1, o_moe_w3, o_moe_w2, final_norm):
    h = x
    for layer in range(DEPTH):
        i = layer // 2
        if layer % 2 == 0:
            h = h + even_mixer(rmsnorm(h, e_norm1[i]), e_w_in[i], e_sgu_ln_g[i], e_sgu_ln_b[i],
                               e_sgu_w[i], e_sgu_b[i], e_conv_w[i], e_a_log[i], e_dt_bias[i],
                               e_o_norm[i], e_w_out[i])
            h = h + swiglu(rmsnorm(h, e_norm2[i]), e_ffn_w1[i], e_ffn_w3[i], e_ffn_w2[i])
        else:
            h = h + stick_breaking_mixer(rmsnorm(h, o_norm1[i]), o_w_qkv[i], o_w_out[i])
            h = h + moe_swiglu(rmsnorm(h, o_norm2[i]), o_router[i], o_moe_w1[i], o_moe_w3[i], o_moe_w2[i])
    return rmsnorm(h, final_norm)
```

```python
import functools

import jax
import jax.numpy as jnp
from jax import lax
from jax.experimental import pallas as pl
from jax.experimental.pallas import tpu as pltpu

F32 = jnp.float32
BF16 = jnp.bfloat16

D_MODEL = 1024
EPS = 1e-6
CHUNK = 64
SGU_BLOCK = 128
SGU_GROUPS = 8
SGU_WIDTH = 512
SGU_GDIM = 64
GDN_HEADS = 4
GDN_DK = 128
GDN_DV = 128
GDN_CONV = 4
GDN_QK = 512
GDN_V = 512
CONV_CH = 1536
SB_HEADS = 16
SB_DIM = 64
D_FF = 2816
N_EXPERTS = 8
LANES = 128
VMEM_LIMIT = 48 * 1024 * 1024


def _params(sem):
    return pltpu.CompilerParams(dimension_semantics=sem, vmem_limit_bytes=VMEM_LIMIT)


def _rms(x, g):
    return x * lax.rsqrt(jnp.mean(x * x, axis=-1, keepdims=True) + EPS) * g


def _sigmoid(x):
    return 1.0 / (1.0 + jnp.exp(-x))


def _softplus(x):
    return jnp.maximum(x, 0.0) + jnp.log1p(jnp.exp(-jnp.abs(x)))


def _gelu(x):
    return 0.5 * x * (1.0 + jnp.tanh(0.7978845608028654 * (x + 0.044715 * (x * x * x))))


def _dot(a, b):
    return jnp.dot(a, b, preferred_element_type=F32)


def _dot_nt(a, b):
    return lax.dot_general(a, b, (((1,), (1,)), ((), ())), preferred_element_type=F32)


def _dot_tn(a, b):
    return lax.dot_general(a, b, (((0,), (0,)), ((), ())), preferred_element_type=F32)


def _norm_proj_kernel(h_ref, g_ref, w_ref, *o_refs, splits):
    xn = _rms(h_ref[...], g_ref[...]).astype(BF16)
    for o_ref, a, b in zip(o_refs, splits[:-1], splits[1:]):
        o_ref[...] = _dot(xn, w_ref[:, a:b]).astype(o_ref.dtype)


def norm_proj(h, g, w, splits, dtypes, tm, name):
    n, d = h.shape
    outs = [jax.ShapeDtypeStruct((n, b - a), dt) for a, b, dt in zip(splits[:-1], splits[1:], dtypes)]
    return pl.pallas_call(
        functools.partial(_norm_proj_kernel, splits=splits),
        grid=(n // tm,),
        in_specs=[pl.BlockSpec((tm, d), lambda i: (i, 0)),
                  pl.BlockSpec((1, d), lambda i: (0, 0)),
                  pl.BlockSpec(w.shape, lambda i: (0, 0))],
        out_specs=[pl.BlockSpec((tm, o.shape[1]), lambda i: (i, 0)) for o in outs],
        out_shape=outs,
        compiler_params=_params(("parallel",)),
        name=name,
    )(h, g.reshape(1, d), w)


def _proj_res_kernel(h_ref, w_ref, *refs, splits):
    x_refs, o_ref = refs[:-1], refs[-1]
    acc = h_ref[...]
    for x_ref, a, b in zip(x_refs, splits[:-1], splits[1:]):
        acc = acc + _dot(x_ref[...], w_ref[a:b, :])
    o_ref[...] = acc


def proj_residual(h, w, xs, tm, name):
    n, d = h.shape
    splits = [0]
    for x in xs:
        splits.append(splits[-1] + x.shape[1])
    return pl.pallas_call(
        functools.partial(_proj_res_kernel, splits=tuple(splits)),
        grid=(n // tm,),
        in_specs=[pl.BlockSpec((tm, d), lambda i: (i, 0)),
                  pl.BlockSpec(w.shape, lambda i: (0, 0))]
                 + [pl.BlockSpec((tm, x.shape[1]), lambda i: (i, 0)) for x in xs],
        out_specs=pl.BlockSpec((tm, d), lambda i: (i, 0)),
        out_shape=jax.ShapeDtypeStruct((n, d), F32),
        compiler_params=_params(("parallel",)),
        name=name,
    )(h, w, *xs)


def _ffn_kernel(h_ref, g_ref, w1_ref, w3_ref, w2_ref, o_ref, xn_ref, acc_ref):
    j = pl.program_id(1)

    @pl.when(j == 0)
    def _():
        xn_ref[...] = _rms(h_ref[...], g_ref[...]).astype(BF16)
        acc_ref[...] = jnp.zeros_like(acc_ref)

    xn = xn_ref[...]
    a = _dot(xn, w1_ref[...])
    b = _dot(xn, w3_ref[...])
    hidden = (a * _sigmoid(a) * b).astype(BF16)
    acc_ref[...] += _dot(hidden, w2_ref[...])

    @pl.when(j == pl.num_programs(1) - 1)
    def _():
        o_ref[...] = h_ref[...] + acc_ref[...]


def ffn_residual(h, g, w1, w3, w2, tm, tf, name):
    n, d = h.shape
    f = w1.shape[1]
    return pl.pallas_call(
        _ffn_kernel,
        grid=(n // tm, f // tf),
        in_specs=[pl.BlockSpec((tm, d), lambda i, j: (i, 0)),
                  pl.BlockSpec((1, d), lambda i, j: (0, 0)),
                  pl.BlockSpec((d, tf), lambda i, j: (0, j)),
                  pl.BlockSpec((d, tf), lambda i, j: (0, j)),
                  pl.BlockSpec((tf, d), lambda i, j: (j, 0))],
        out_specs=pl.BlockSpec((tm, d), lambda i, j: (i, 0)),
        out_shape=jax.ShapeDtypeStruct((n, d), F32),
        scratch_shapes=[pltpu.VMEM((tm, d), BF16), pltpu.VMEM((tm, d), F32)],
        compiler_params=_params(("parallel", "arbitrary")),
        name=name,
    )(h, g.reshape(1, d), w1, w3, w2)


def _router_kernel(h_ref, g_ref, wr_ref, cw_ref):
    xn = _rms(h_ref[...], g_ref[...])
    logits = jnp.dot(xn, wr_ref[...], preferred_element_type=F32,
                     precision=lax.Precision.HIGHEST)
    lane = lax.broadcasted_iota(jnp.int32, logits.shape, 1)
    neg = jnp.float32(-jnp.inf)
    l1 = jnp.where(lane < N_EXPERTS, logits, neg)
    m1 = jnp.max(l1, axis=-1, keepdims=True)
    i1 = jnp.min(jnp.where(l1 == m1, lane, LANES), axis=-1, keepdims=True)
    l2 = jnp.where(lane == i1, neg, l1)
    m2 = jnp.max(l2, axis=-1, keepdims=True)
    i2 = jnp.min(jnp.where(l2 == m2, lane, LANES), axis=-1, keepdims=True)
    e2 = jnp.exp(m2 - m1)
    g1 = 1.0 / (1.0 + e2)
    g2 = e2 / (1.0 + e2)
    cw_ref[...] = jnp.where(lane == i1, g1, 0.0) + jnp.where(lane == i2, g2, 0.0)


def router(h, g, wr_pad, tm, name):
    n, d = h.shape
    return pl.pallas_call(
        _router_kernel,
        grid=(n // tm,),
        in_specs=[pl.BlockSpec((tm, d), lambda i: (i, 0)),
                  pl.BlockSpec((1, d), lambda i: (0, 0)),
                  pl.BlockSpec((d, LANES), lambda i: (0, 0))],
        out_specs=pl.BlockSpec((tm, LANES), lambda i: (i, 0)),
        out_shape=jax.ShapeDtypeStruct((n, LANES), F32),
        compiler_params=_params(("parallel",)),
        name=name,
    )(h, g.reshape(1, d), wr_pad)


def _moe_kernel(h_ref, g_ref, cw_ref, w1_ref, w3_ref, w2_ref, o_ref, xn_ref, acc_ref):
    e = pl.program_id(1)
    j = pl.program_id(2)

    @pl.when((e == 0) & (j == 0))
    def _():
        xn_ref[...] = _rms(h_ref[...], g_ref[...]).astype(BF16)
        acc_ref[...] = jnp.zeros_like(acc_ref)

    cw = cw_ref[...]
    lane = lax.broadcasted_iota(jnp.int32, cw.shape, 1)
    cwe = jnp.sum(jnp.where(lane == e, cw, 0.0), axis=-1, keepdims=True)
    xn = xn_ref[...]
    a = _dot(xn, w1_ref[...])
    b = _dot(xn, w3_ref[...])
    hidden = (a * _sigmoid(a) * b * cwe).astype(BF16)
    acc_ref[...] += _dot(hidden, w2_ref[...])

    @pl.when((e == pl.num_programs(1) - 1) & (j == pl.num_programs(2) - 1))
    def _():
        o_ref[...] = h_ref[...] + acc_ref[...]


def moe_residual(h, g, cw, w1, w3, w2, tm, tf, name):
    n, d = h.shape
    ne, _, f = w1.shape
    return pl.pallas_call(
        _moe_kernel,
        grid=(n // tm, ne, f // tf),
        in_specs=[pl.BlockSpec((tm, d), lambda i, e, j: (i, 0)),
                  pl.BlockSpec((1, d), lambda i, e, j: (0, 0)),
                  pl.BlockSpec((tm, LANES), lambda i, e, j: (i, 0)),
                  pl.BlockSpec((None, d, tf), lambda i, e, j: (e, 0, j)),
                  pl.BlockSpec((None, d, tf), lambda i, e, j: (e, 0, j)),
                  pl.BlockSpec((None, tf, d), lambda i, e, j: (e, j, 0))],
        out_specs=pl.BlockSpec((tm, d), lambda i, e, j: (i, 0)),
        out_shape=jax.ShapeDtypeStruct((n, d), F32),
        scratch_shapes=[pltpu.VMEM((tm, d), BF16), pltpu.VMEM((tm, d), F32)],
        compiler_params=_params(("parallel", "arbitrary", "arbitrary")),
        name=name,
    )(h, g.reshape(1, d), cw, w1, w3, w2)


def _final_norm_kernel(h_ref, g_ref, o_ref):
    o_ref[...] = _rms(h_ref[...], g_ref[...])


def final_norm(h, g, tm, name):
    n, d = h.shape
    return pl.pallas_call(
        _final_norm_kernel,
        grid=(n // tm,),
        in_specs=[pl.BlockSpec((tm, d), lambda i: (i, 0)),
                  pl.BlockSpec((1, d), lambda i: (0, 0))],
        out_specs=pl.BlockSpec((tm, d), lambda i: (i, 0)),
        out_shape=jax.ShapeDtypeStruct((n, d), F32),
        compiler_params=_params(("parallel",)),
        name=name,
    )(h, g.reshape(1, d))


def _sgu_kernel(uv_ref, lng_ref, lnb_ref, wcat_ref, bias_ref, o_ref):
    uv = uv_ref[...]
    u = _gelu(uv[:, :SGU_WIDTH])
    v = _gelu(uv[:, SGU_WIDTH:])
    mu = jnp.mean(v, axis=-1, keepdims=True)
    vc = v - mu
    var = jnp.mean(vc * vc, axis=-1, keepdims=True)
    v = (vc * lax.rsqrt(var + EPS) * lng_ref[...] + lnb_ref[...]).astype(BF16)

    kdim = SGU_GROUPS * SGU_BLOCK
    row_g = lax.broadcasted_iota(jnp.int32, (kdim, SGU_WIDTH), 0) // SGU_BLOCK
    col_g = lax.broadcasted_iota(jnp.int32, (kdim, SGU_WIDTH), 1) // SGU_GDIM
    vbd = jnp.where(row_g == col_g, jnp.concatenate([v] * SGU_GROUPS, axis=0), jnp.zeros((), BF16))

    t_chunk = lax.broadcasted_iota(jnp.int32, (SGU_BLOCK, kdim), 0) // CHUNK
    s_chunk = (lax.broadcasted_iota(jnp.int32, (SGU_BLOCK, kdim), 1) % SGU_BLOCK) // CHUNK
    w = jnp.where(s_chunk <= t_chunk, wcat_ref[...], 0.0).astype(BF16)
    mixed = _dot(w, vbd) + bias_ref[...]
    o_ref[...] = (u * mixed).astype(o_ref.dtype)


def sgu(uv, ln_g, ln_b, w_s, b_s, name):
    n = uv.shape[0]
    wcat = jnp.transpose(w_s, (1, 0, 2)).reshape(SGU_BLOCK, SGU_GROUPS * SGU_BLOCK)
    bias = jnp.repeat(b_s.T, SGU_GDIM, axis=1)
    return pl.pallas_call(
        _sgu_kernel,
        grid=(n // SGU_BLOCK,),
        in_specs=[pl.BlockSpec((SGU_BLOCK, 2 * SGU_WIDTH), lambda i: (i, 0)),
                  pl.BlockSpec((1, SGU_WIDTH), lambda i: (0, 0)),
                  pl.BlockSpec((1, SGU_WIDTH), lambda i: (0, 0)),
                  pl.BlockSpec(wcat.shape, lambda i: (0, 0)),
                  pl.BlockSpec(bias.shape, lambda i: (0, 0))],
        out_specs=pl.BlockSpec((SGU_BLOCK, SGU_WIDTH), lambda i: (i, 0)),
        out_shape=jax.ShapeDtypeStruct((n, SGU_WIDTH), BF16),
        compiler_params=_params(("parallel",)),
        name=name,
    )(uv, ln_g.reshape(1, -1), ln_b.reshape(1, -1), wcat, bias)


GDN_T = 256
HALO = 8


def _gdn_kernel(qkv_ref, gate_ref, ba_ref, bat_ref, convw_ref, arow_ref, acol_ref, onorm_ref,
                o_ref, state_ref, xbuf_ref):
    tb = pl.program_id(1)

    @pl.when(tb == 0)
    def _():
        state_ref[...] = jnp.zeros_like(state_ref)
        xbuf_ref[0:HALO, :] = jnp.zeros((HALO, CONV_CH), F32)

    @pl.when(tb != 0)
    def _():
        xbuf_ref[0:HALO, :] = xbuf_ref[GDN_T:GDN_T + HALO, :]

    xbuf_ref[HALO:HALO + GDN_T, :] = qkv_ref[...]
    cw = convw_ref[...]
    y = cw[0:1, :] * xbuf_ref[pl.ds(HALO - 3, GDN_T), :]
    for i in range(1, GDN_CONV):
        y = y + cw[i:i + 1, :] * xbuf_ref[pl.ds(HALO - 3 + i, GDN_T), :]
    qkv = y * _sigmoid(y)

    arow = arow_ref[...]
    acol = acol_ref[...]
    ba = ba_ref[...]
    beta_full = _sigmoid(ba)
    g_full = arow[0:1, :] * _softplus(ba + arow[1:2, :])
    bat = bat_ref[...]
    gt_full = acol[:, 0:1] * _softplus(bat + acol[:, 1:2])

    ii = lax.broadcasted_iota(jnp.int32, (CHUNK, CHUNK), 0)
    jj = lax.broadcasted_iota(jnp.int32, (CHUNK, CHUNK), 1)
    causal = ii >= jj
    strict = ii > jj
    onorm = onorm_ref[...]
    gate = gate_ref[...]

    for h in range(GDN_HEADS):
        state = state_ref[h]
        qh = qkv[:, h * GDN_DK:(h + 1) * GDN_DK]
        kh = qkv[:, GDN_QK + h * GDN_DK:GDN_QK + (h + 1) * GDN_DK]
        vh = qkv[:, 2 * GDN_QK + h * GDN_DV:2 * GDN_QK + (h + 1) * GDN_DV]
        qh = qh * lax.rsqrt(jnp.sum(qh * qh, axis=-1, keepdims=True) + EPS) * (GDN_DK ** -0.5)
        kh = kh * lax.rsqrt(jnp.sum(kh * kh, axis=-1, keepdims=True) + EPS)
        for c in range(GDN_T // CHUNK):
            r0 = c * CHUNK
            q = qh[r0:r0 + CHUNK]
            k = kh[r0:r0 + CHUNK]
            v = vh[r0:r0 + CHUNK]
            beta = beta_full[r0:r0 + CHUNK, h:h + 1]
            g_col = g_full[r0:r0 + CHUNK, 4 + h:5 + h]
            g_row = gt_full[4 + h:5 + h, r0:r0 + CHUNK]
            gc_col = jnp.sum(jnp.where(causal, g_row, 0.0), axis=1, keepdims=True)
            gc_row = jnp.sum(jnp.where(ii <= jj, g_col, 0.0), axis=0, keepdims=True)
            g_last = jnp.sum(g_row, axis=1, keepdims=True)
            gamma = jnp.where(causal, jnp.exp(jnp.where(causal, gc_col - gc_row, 0.0)), 0.0)
            egc = jnp.exp(gc_col)
            kb = k * beta
            kbf = k.astype(BF16)
            lmat = jnp.where(strict, _dot_nt(kb.astype(BF16), kbf) * gamma, 0.0)
            attn = _dot_nt(q.astype(BF16), kbf) * gamma
            x = jnp.concatenate([v * beta, kb * egc], axis=1)
            npow = -lmat
            for level in range(6):
                nb = npow.astype(BF16)
                x = x + _dot(nb, x.astype(BF16))
                if level < 5:
                    npow = _dot(nb, nb)
            u = x[:, :GDN_DV]
            w = x[:, GDN_DV:]
            wq = jnp.concatenate([w, q * egc], axis=0).astype(BF16)
            ws = _dot(wq, state.astype(BF16))
            v_new = u - ws[:CHUNK]
            o = ws[CHUNK:] + _dot(attn.astype(BF16), v_new.astype(BF16))
            k_dec = k * jnp.exp(g_last - gc_col)
            state = state * jnp.exp(g_last) + _dot_tn(k_dec.astype(BF16), v_new.astype(BF16))
            o = o * lax.rsqrt(jnp.mean(o * o, axis=-1, keepdims=True) + EPS) * onorm
            gt = gate[r0:r0 + CHUNK, h * GDN_DV:(h + 1) * GDN_DV]
            o_ref[r0:r0 + CHUNK, h * GDN_DV:(h + 1) * GDN_DV] = (o * (gt * _sigmoid(gt))).astype(o_ref.dtype)
        state_ref[h] = state


def gdn(qkv, gate, ba, bat, conv_w, a_log, dt_bias, o_norm, batch, name):
    n = qkv.shape[0]
    s = n // batch
    nt = s // GDN_T
    neg_a = -jnp.exp(a_log)
    arow = jnp.zeros((8, LANES), F32).at[0, 4:8].set(neg_a).at[1, 4:8].set(dt_bias)
    acol = jnp.zeros((8, LANES), F32).at[4:8, 0].set(neg_a).at[4:8, 1].set(dt_bias)
    return pl.pallas_call(
        _gdn_kernel,
        grid=(batch, nt),
        in_specs=[pl.BlockSpec((GDN_T, CONV_CH), lambda b, t: (b * nt + t, 0)),
                  pl.BlockSpec((GDN_T, GDN_V), lambda b, t: (b * nt + t, 0)),
                  pl.BlockSpec((GDN_T, LANES), lambda b, t: (b * nt + t, 0)),
                  pl.BlockSpec((8, GDN_T), lambda b, t: (0, b * nt + t)),
                  pl.BlockSpec((GDN_CONV, CONV_CH), lambda b, t: (0, 0)),
                  pl.BlockSpec((8, LANES), lambda b, t: (0, 0)),
                  pl.BlockSpec((8, LANES), lambda b, t: (0, 0)),
                  pl.BlockSpec((1, GDN_DV), lambda b, t: (0, 0))],
        out_specs=pl.BlockSpec((GDN_T, GDN_V), lambda b, t: (b * nt + t, 0)),
        out_shape=jax.ShapeDtypeStruct((n, GDN_V), BF16),
        scratch_shapes=[pltpu.VMEM((GDN_HEADS, GDN_DK, GDN_DV), F32),
                        pltpu.VMEM((HALO + GDN_T, CONV_CH), F32)],
        compiler_params=_params(("parallel", "arbitrary")),
        name=name,
    )(qkv, gate, ba, bat, conv_w, arow, acol, o_norm.reshape(1, -1))


SB_T = 256


def _sb_kernel(q_ref, k_ref, v_ref, o_ref):
    qi = pl.program_id(2)
    q = q_ref[...]
    lane = lax.broadcasted_iota(jnp.int32, (1, LANES), 1)
    row = lax.broadcasted_iota(jnp.int32, (SB_T, SB_T), 0)
    col = lax.broadcasted_iota(jnp.int32, (SB_T, SB_T), 1)
    earlier = col < row
    later_key = (row > col).astype(BF16)

    def tile(qh, kb, carry, diagonal):
        acc, csum = carry
        start = pl.multiple_of(kb * SB_T, SB_T)
        kblk = k_ref[pl.ds(start, SB_T), :]
        vblk = v_ref[pl.ds(start, SB_T), :]
        z = _dot_nt(qh, kblk)
        lk = jnp.minimum(-z, 0.0) - jnp.log1p(jnp.exp(-jnp.abs(z)))
        if diagonal:
            lk = jnp.where(earlier, lk, 0.0)
        after = _dot(lk.astype(BF16), later_key) + csum
        att = jnp.exp(z + lk + after)
        if diagonal:
            att = jnp.where(earlier, att, 0.0)
        acc = acc + _dot(att.astype(BF16), vblk)
        csum = csum + jnp.sum(lk, axis=-1, keepdims=True)
        return acc, csum

    outs = []
    for hh in range(2):
        in_head = (lane >= hh * SB_DIM) & (lane < (hh + 1) * SB_DIM)
        qh = jnp.where(in_head, q, jnp.zeros((), BF16)) * jnp.asarray(SB_DIM ** -0.5, BF16)
        carry = (jnp.zeros((SB_T, LANES), F32), jnp.zeros((SB_T, 1), F32))
        carry = tile(qh, qi, carry, True)
        carry = lax.fori_loop(0, qi, lambda i, c: tile(qh, qi - 1 - i, c, False), carry)
        outs.append(carry[0])
    o_ref[...] = jnp.where(lane < SB_DIM, outs[0], outs[1]).astype(o_ref.dtype)


def stick_breaking(q, k, v, batch, name):
    n, d = q.shape
    s = n // batch
    q3, k3, v3 = (t.reshape(batch, s, d) for t in (q, k, v))
    out = pl.pallas_call(
        _sb_kernel,
        grid=(batch, d // LANES, s // SB_T),
        in_specs=[pl.BlockSpec((None, SB_T, LANES), lambda b, p, i: (b, i, p)),
                  pl.BlockSpec((None, s, LANES), lambda b, p, i: (b, 0, p)),
                  pl.BlockSpec((None, s, LANES), lambda b, p, i: (b, 0, p))],
        out_specs=pl.BlockSpec((None, SB_T, LANES), lambda b, p, i: (b, i, p)),
        out_shape=jax.ShapeDtypeStruct((batch, s, d), BF16),
        compiler_params=_params(("parallel", "parallel", "arbitrary")),
        name=name,
    )(q3, k3, v3)
    return out.reshape(n, d)


def kernel(x, e_norm1, e_w_in, e_sgu_ln_g, e_sgu_ln_b, e_sgu_w, e_sgu_b, e_conv_w, e_a_log, e_dt_bias, e_o_norm, e_w_out, e_norm2, e_ffn_w1, e_ffn_w3, e_ffn_w2, o_norm1, o_w_qkv, o_w_out, o_norm2, o_router, o_moe_w1, o_moe_w3, o_moe_w2, final_norm):
    batch, seq, d = x.shape
    n = batch * seq
    h = x.reshape(n, d)
    n_main = 2 * SGU_WIDTH + CONV_CH + GDN_V
    depth = e_norm1.shape[0] + o_norm1.shape[0]
    for layer in range(depth):
        i = layer // 2
        if layer % 2 == 0:
            w_in = e_w_in[i]
            w_in = jnp.concatenate(
                [w_in, jnp.zeros((d, LANES - (w_in.shape[1] - n_main)), F32)], axis=1).astype(BF16)
            splits = (0, 2 * SGU_WIDTH, 2 * SGU_WIDTH + CONV_CH, n_main, n_main + LANES)
            uv, qkv, gate, ba = norm_proj(h, e_norm1[i], w_in, splits, (F32, F32, F32, F32), 512,
                                          f"even{i}_in_proj")
            y_a = sgu(uv, e_sgu_ln_g[i], e_sgu_ln_b[i], e_sgu_w[i], e_sgu_b[i], f"even{i}_sgu")
            y_b = gdn(qkv, gate, ba, ba[:, :8].T, e_conv_w[i], e_a_log[i], e_dt_bias[i], e_o_norm[i],
                      batch, f"even{i}_gdn")
            h = proj_residual(h, e_w_out[i].astype(BF16), [y_a, y_b], 1024, f"even{i}_out_proj")
            h = ffn_residual(h, e_norm2[i], e_ffn_w1[i].astype(BF16), e_ffn_w3[i].astype(BF16),
                             e_ffn_w2[i].astype(BF16), 1024, 256, f"even{i}_ffn")
        else:
            q, k, v = norm_proj(h, o_norm1[i], o_w_qkv[i].astype(BF16), (0, d, 2 * d, 3 * d),
                                (BF16, BF16, BF16), 512, f"odd{i}_qkv_proj")
            att = stick_breaking(q, k, v, batch, f"odd{i}_stick_breaking")
            h = proj_residual(h, o_w_out[i].astype(BF16), [att], 1024, f"odd{i}_out_proj")
            wr = jnp.concatenate([o_router[i], jnp.zeros((d, LANES - N_EXPERTS), F32)], axis=1)
            cw = router(h, o_norm2[i], wr, 1024, f"odd{i}_router")
            h = moe_residual(h, o_norm2[i], cw, o_moe_w1[i].astype(BF16), o_moe_w3[i].astype(BF16),
                             o_moe_w2[i].astype(BF16), 1024, 256, f"odd{i}_moe")
    return final_norm_call(h, final_norm).reshape(batch, seq, d)


def final_norm_call(h, g):
    return final_norm(h, g, 1024, "final_norm")
```

```python
import functools

import jax
import jax.numpy as jnp
from jax import lax
from jax.experimental import pallas as pl
from jax.experimental.pallas import tpu as pltpu

F32 = jnp.float32
BF16 = jnp.bfloat16

D_MODEL = 1024
EPS = 1e-6
CHUNK = 64
SGU_BLOCK = 128
SGU_GROUPS = 8
SGU_WIDTH = 512
SGU_GDIM = 64
GDN_HEADS = 4
GDN_DK = 128
GDN_DV = 128
GDN_CONV = 4
GDN_QK = 512
GDN_V = 512
CONV_CH = 1536
SB_HEADS = 16
SB_DIM = 64
D_FF = 2816
N_EXPERTS = 8
LANES = 128
VMEM_LIMIT = 48 * 1024 * 1024


def _params(sem):
    return pltpu.CompilerParams(dimension_semantics=sem, vmem_limit_bytes=VMEM_LIMIT)


def _rms(x, g):
    return x * lax.rsqrt(jnp.mean(x * x, axis=-1, keepdims=True) + EPS) * g


def _sigmoid(x):
    return 1.0 / (1.0 + jnp.exp(-x))


def _softplus(x):
    return jnp.maximum(x, 0.0) + jnp.log1p(jnp.exp(-jnp.abs(x)))


def _gelu(x):
    return 0.5 * x * (1.0 + jnp.tanh(0.7978845608028654 * (x + 0.044715 * (x * x * x))))


def _dot(a, b):
    return jnp.dot(a, b, preferred_element_type=F32)


def _dot_nt(a, b):
    return lax.dot_general(a, b, (((1,), (1,)), ((), ())), preferred_element_type=F32)


def _dot_tn(a, b):
    return lax.dot_general(a, b, (((0,), (0,)), ((), ())), preferred_element_type=F32)


def _norm_proj_kernel(h_ref, g_ref, w_ref, *o_refs, splits):
    xn = _rms(h_ref[...], g_ref[...]).astype(BF16)
    for o_ref, a, b in zip(o_refs, splits[:-1], splits[1:]):
        o_ref[...] = _dot(xn, w_ref[:, a:b]).astype(o_ref.dtype)


def norm_proj(h, g, w, splits, dtypes, tm, name):
    n, d = h.shape
    outs = [jax.ShapeDtypeStruct((n, b - a), dt) for a, b, dt in zip(splits[:-1], splits[1:], dtypes)]
    return pl.pallas_call(
        functools.partial(_norm_proj_kernel, splits=splits),
        grid=(n // tm,),
        in_specs=[pl.BlockSpec((tm, d), lambda i: (i, 0)),
                  pl.BlockSpec((1, d), lambda i: (0, 0)),
                  pl.BlockSpec(w.shape, lambda i: (0, 0))],
        out_specs=[pl.BlockSpec((tm, o.shape[1]), lambda i: (i, 0)) for o in outs],
        out_shape=outs,
        compiler_params=_params(("parallel",)),
        name=name,
    )(h, g.reshape(1, d), w)


def _proj_res_kernel(h_ref, w_ref, *refs, splits):
    x_refs, o_ref = refs[:-1], refs[-1]
    acc = h_ref[...]
    for x_ref, a, b in zip(x_refs, splits[:-1], splits[1:]):
        acc = acc + _dot(x_ref[...], w_ref[a:b, :])
    o_ref[...] = acc


def proj_residual(h, w, xs, tm, name):
    n, d = h.shape
    splits = [0]
    for x in xs:
        splits.append(splits[-1] + x.shape[1])
    return pl.pallas_call(
        functools.partial(_proj_res_kernel, splits=tuple(splits)),
        grid=(n // tm,),
        in_specs=[pl.BlockSpec((tm, d), lambda i: (i, 0)),
                  pl.BlockSpec(w.shape, lambda i: (0, 0))]
                 + [pl.BlockSpec((tm, x.shape[1]), lambda i: (i, 0)) for x in xs],
        out_specs=pl.BlockSpec((tm, d), lambda i: (i, 0)),
        out_shape=jax.ShapeDtypeStruct((n, d), F32),
        compiler_params=_params(("parallel",)),
        name=name,
    )(h, w, *xs)


def _ffn_kernel(h_ref, g_ref, w1_ref, w3_ref, w2_ref, o_ref, xn_ref, acc_ref):
    j = pl.program_id(1)

    @pl.when(j == 0)
    def _():
        xn_ref[...] = _rms(h_ref[...], g_ref[...]).astype(BF16)
        acc_ref[...] = jnp.zeros_like(acc_ref)

    xn = xn_ref[...]
    a = _dot(xn, w1_ref[...])
    b = _dot(xn, w3_ref[...])
    hidden = (a * _sigmoid(a) * b).astype(BF16)
    acc_ref[...] += _dot(hidden, w2_ref[...])

    @pl.when(j == pl.num_programs(1) - 1)
    def _():
        o_ref[...] = h_ref[...] + acc_ref[...]


def ffn_residual(h, g, w1, w3, w2, tm, tf, name):
    n, d = h.shape
    f = w1.shape[1]
    return pl.pallas_call(
        _ffn_kernel,
        grid=(n // tm, f // tf),
        in_specs=[pl.BlockSpec((tm, d), lambda i, j: (i, 0)),
                  pl.BlockSpec((1, d), lambda i, j: (0, 0)),
                  pl.BlockSpec((d, tf), lambda i, j: (0, j)),
                  pl.BlockSpec((d, tf), lambda i, j: (0, j)),
                  pl.BlockSpec((tf, d), lambda i, j: (j, 0))],
        out_specs=pl.BlockSpec((tm, d), lambda i, j: (i, 0)),
        out_shape=jax.ShapeDtypeStruct((n, d), F32),
        scratch_shapes=[pltpu.VMEM((tm, d), BF16), pltpu.VMEM((tm, d), F32)],
        compiler_params=_params(("parallel", "arbitrary")),
        name=name,
    )(h, g.reshape(1, d), w1, w3, w2)


MOE_TM = 512
ROUTER_TM = 256
META_E1, META_E2, META_R1, META_R2, META_G1, META_G2 = range(6)


def _router_kernel(h_ref, g_ref, wr_ref, meta_ref, cnt_ref, run_ref):
    @pl.when(pl.program_id(0) == 0)
    def _():
        run_ref[...] = jnp.zeros_like(run_ref)

    xn = _rms(h_ref[...], g_ref[...])
    logits = jnp.dot(xn, wr_ref[...], preferred_element_type=F32,
                     precision=lax.Precision.HIGHEST)
    tm = logits.shape[0]
    lane = lax.broadcasted_iota(jnp.int32, logits.shape, 1)
    neg = jnp.float32(-jnp.inf)
    l1 = jnp.where(lane < N_EXPERTS, logits, neg)
    m1 = jnp.max(l1, axis=-1, keepdims=True)
    i1 = jnp.min(jnp.where(l1 == m1, lane, LANES), axis=-1, keepdims=True)
    l2 = jnp.where(lane == i1, neg, l1)
    m2 = jnp.max(l2, axis=-1, keepdims=True)
    i2 = jnp.min(jnp.where(l2 == m2, lane, LANES), axis=-1, keepdims=True)
    e2 = jnp.exp(m2 - m1)
    g1 = 1.0 / (1.0 + e2)
    g2 = e2 / (1.0 + e2)

    chosen = (lane == i1) | (lane == i2)
    onehot = jnp.where(chosen, 1.0, 0.0)
    rr = lax.broadcasted_iota(jnp.int32, (tm, tm), 0)
    cc = lax.broadcasted_iota(jnp.int32, (tm, tm), 1)
    before = jnp.where(cc < rr, 1.0, 0.0).astype(BF16)
    prefix = _dot(before, onehot.astype(BF16)) + run_ref[0:1, :]
    r1 = jnp.sum(jnp.where(lane == i1, prefix, 0.0), axis=-1, keepdims=True)
    r2 = jnp.sum(jnp.where(lane == i2, prefix, 0.0), axis=-1, keepdims=True)
    run_ref[...] = run_ref[...] + jnp.sum(onehot, axis=0, keepdims=True)
    cnt_ref[...] = run_ref[...]

    meta = jnp.zeros(logits.shape, F32)
    for idx, val in ((META_E1, i1.astype(F32)), (META_E2, i2.astype(F32)), (META_R1, r1), (META_R2, r2),
                     (META_G1, g1), (META_G2, g2)):
        meta = jnp.where(lane == idx, val, meta)
    meta_ref[...] = meta


def router(h, g, wr_pad, name):
    n, d = h.shape
    tm = ROUTER_TM
    return pl.pallas_call(
        _router_kernel,
        grid=(n // tm,),
        in_specs=[pl.BlockSpec((tm, d), lambda i: (i, 0)),
                  pl.BlockSpec((1, d), lambda i: (0, 0)),
                  pl.BlockSpec((d, LANES), lambda i: (0, 0))],
        out_specs=[pl.BlockSpec((tm, LANES), lambda i: (i, 0)),
                   pl.BlockSpec((8, LANES), lambda i: (0, 0))],
        out_shape=[jax.ShapeDtypeStruct((n, LANES), F32), jax.ShapeDtypeStruct((8, LANES), F32)],
        scratch_shapes=[pltpu.VMEM((8, LANES), F32)],
        compiler_params=_params(("arbitrary",)),
        name=name,
    )(h, g.reshape(1, d), wr_pad)


def _dispatch_kernel(pos1_ref, pos2_ref, h_hbm, xs_init_hbm, xs_hbm, sem, *, tm):
    del xs_init_hbm
    base = pl.program_id(0) * tm

    def copies(r):
        src = h_hbm.at[pl.ds(base + r, 1)]
        return (pltpu.make_async_copy(src, xs_hbm.at[pl.ds(pos1_ref[r], 1)], sem.at[0]),
                pltpu.make_async_copy(src, xs_hbm.at[pl.ds(pos2_ref[r], 1)], sem.at[1]))

    def start(r, carry):
        for cp in copies(r):
            cp.start()
        return carry

    def wait(r, carry):
        for cp in copies(r):
            cp.wait()
        return carry

    lax.fori_loop(0, tm, start, 0)
    lax.fori_loop(0, tm, wait, 0)


def dispatch(h, pos1, pos2, n_rows, tm, name):
    n, d = h.shape
    return pl.pallas_call(
        functools.partial(_dispatch_kernel, tm=tm),
        grid=(n // tm,),
        in_specs=[pl.BlockSpec((tm,), lambda i: (i,), memory_space=pltpu.SMEM),
                  pl.BlockSpec((tm,), lambda i: (i,), memory_space=pltpu.SMEM),
                  pl.BlockSpec(memory_space=pl.ANY),
                  pl.BlockSpec(memory_space=pl.ANY)],
        out_specs=pl.BlockSpec(memory_space=pl.ANY),
        out_shape=jax.ShapeDtypeStruct((n_rows, d), F32),
        scratch_shapes=[pltpu.SemaphoreType.DMA((2,))],
        input_output_aliases={3: 0},
        compiler_params=_params(("arbitrary",)),
        name=name,
    )(pos1, pos2, h, jnp.zeros((n_rows, d), F32))


def _experts_kernel(te_ref, nt_ref, xs_ref, g_ref, w1_ref, w3_ref, w2_ref, ys_ref, xn_ref, acc_ref):
    del te_ref
    i = pl.program_id(0)
    j = pl.program_id(1)

    @pl.when(i < nt_ref[0])
    def _():
        @pl.when(j == 0)
        def _():
            xn_ref[...] = _rms(xs_ref[...], g_ref[...]).astype(BF16)
            acc_ref[...] = jnp.zeros_like(acc_ref)

        xn = xn_ref[...]
        a = _dot(xn, w1_ref[...])
        b = _dot(xn, w3_ref[...])
        hidden = (a * _sigmoid(a) * b).astype(BF16)
        acc_ref[...] += _dot(hidden, w2_ref[...])

        @pl.when(j == pl.num_programs(1) - 1)
        def _():
            ys_ref[...] = acc_ref[...]

    @pl.when((i >= nt_ref[0]) & (j == pl.num_programs(1) - 1))
    def _():
        ys_ref[...] = jnp.zeros_like(ys_ref)


def experts(xs, g, tile_expert, n_tiles, w1, w3, w2, tf, name):
    n_rows, d = xs.shape
    f = w1.shape[2]
    tm = MOE_TM
    nj = f // tf

    def row_map(i, j, te, nt):
        return (jnp.minimum(i, nt[0] - 1), 0)

    def up_map(i, j, te, nt):
        return (te[jnp.minimum(i, nt[0] - 1)], 0, jnp.where(i < nt[0], j, nj - 1))

    def down_map(i, j, te, nt):
        return (te[jnp.minimum(i, nt[0] - 1)], jnp.where(i < nt[0], j, nj - 1), 0)

    return pl.pallas_call(
        _experts_kernel,
        grid_spec=pltpu.PrefetchScalarGridSpec(
            num_scalar_prefetch=2,
            grid=(n_rows // tm, nj),
            in_specs=[pl.BlockSpec((tm, d), row_map),
                      pl.BlockSpec((1, d), lambda i, j, te, nt: (0, 0)),
                      pl.BlockSpec((None, d, tf), up_map),
                      pl.BlockSpec((None, d, tf), up_map),
                      pl.BlockSpec((None, tf, d), down_map)],
            out_specs=pl.BlockSpec((tm, d), lambda i, j, te, nt: (i, 0)),
            scratch_shapes=[pltpu.VMEM((tm, d), BF16), pltpu.VMEM((tm, d), F32)]),
        out_shape=jax.ShapeDtypeStruct((n_rows, d), F32),
        compiler_params=_params(("arbitrary", "arbitrary")),
        name=name,
    )(tile_expert, n_tiles, xs, g.reshape(1, d), w1, w3, w2)


def _combine_kernel(pos1_ref, pos2_ref, meta_ref, h_ref, gf_ref, ys_hbm, o_ref, buf_ref, sem, *, tm, final):
    def copies(r):
        return (pltpu.make_async_copy(ys_hbm.at[pl.ds(pos1_ref[r], 1)], buf_ref.at[0, pl.ds(r, 1)], sem.at[0]),
                pltpu.make_async_copy(ys_hbm.at[pl.ds(pos2_ref[r], 1)], buf_ref.at[1, pl.ds(r, 1)], sem.at[1]))

    def start(r, carry):
        for cp in copies(r):
            cp.start()
        return carry

    def wait(r, carry):
        for cp in copies(r):
            cp.wait()
        return carry

    lax.fori_loop(0, tm, start, 0)
    lax.fori_loop(0, tm, wait, 0)
    meta = meta_ref[...]
    out = (h_ref[...] + meta[:, META_G1:META_G1 + 1] * buf_ref[0]
           + meta[:, META_G2:META_G2 + 1] * buf_ref[1])
    if final:
        out = _rms(out, gf_ref[...])
    o_ref[...] = out


def combine(h, meta, pos1, pos2, ys, final_g, tm, name):
    n, d = h.shape
    final = final_g is not None
    gf = (final_g if final else jnp.ones((d,), F32)).reshape(1, d)
    return pl.pallas_call(
        functools.partial(_combine_kernel, tm=tm, final=final),
        grid=(n // tm,),
        in_specs=[pl.BlockSpec((tm,), lambda i: (i,), memory_space=pltpu.SMEM),
                  pl.BlockSpec((tm,), lambda i: (i,), memory_space=pltpu.SMEM),
                  pl.BlockSpec((tm, LANES), lambda i: (i, 0)),
                  pl.BlockSpec((tm, d), lambda i: (i, 0)),
                  pl.BlockSpec((1, d), lambda i: (0, 0)),
                  pl.BlockSpec(memory_space=pl.ANY)],
        out_specs=pl.BlockSpec((tm, d), lambda i: (i, 0)),
        out_shape=jax.ShapeDtypeStruct((n, d), F32),
        scratch_shapes=[pltpu.VMEM((2, tm, d), F32), pltpu.SemaphoreType.DMA((2,))],
        compiler_params=_params(("arbitrary",)),
        name=name,
    )(pos1, pos2, meta, h, gf, ys)


def moe_residual(h, g, w_router, w1, w3, w2, final_g, name):
    n, d = h.shape
    wr = jnp.concatenate([w_router, jnp.zeros((d, LANES - N_EXPERTS), F32)], axis=1)
    meta, cnt = router(h, g, wr, f"{name}_router")
    cnt = cnt[0, :N_EXPERTS].astype(jnp.int32)
    n_tile_e = (cnt + MOE_TM - 1) // MOE_TM
    tile_end = jnp.cumsum(n_tile_e)
    offset = (tile_end - n_tile_e) * MOE_TM
    e1 = meta[:, META_E1].astype(jnp.int32)
    e2 = meta[:, META_E2].astype(jnp.int32)
    pos1 = offset[e1] + meta[:, META_R1].astype(jnp.int32)
    pos2 = offset[e2] + meta[:, META_R2].astype(jnp.int32)
    max_tiles = (2 * n) // MOE_TM + N_EXPERTS
    tile_ids = jnp.arange(max_tiles, dtype=jnp.int32)
    tile_expert = jnp.minimum(jnp.sum(tile_ids[:, None] >= tile_end[None, :], axis=1), N_EXPERTS - 1)
    n_tiles = tile_end[-1:].astype(jnp.int32)
    xs = dispatch(h, pos1, pos2, max_tiles * MOE_TM, 1024, f"{name}_dispatch")
    ys = experts(xs, g, tile_expert.astype(jnp.int32), n_tiles, w1, w3, w2, 256, f"{name}_experts")
    return combine(h, meta, pos1, pos2, ys, final_g, 512, f"{name}_combine")


def _final_norm_kernel(h_ref, g_ref, o_ref):
    o_ref[...] = _rms(h_ref[...], g_ref[...])


def final_norm_call(h, g, tm, name):
    n, d = h.shape
    return pl.pallas_call(
        _final_norm_kernel,
        grid=(n // tm,),
        in_specs=[pl.BlockSpec((tm, d), lambda i: (i, 0)),
                  pl.BlockSpec((1, d), lambda i: (0, 0))],
        out_specs=pl.BlockSpec((tm, d), lambda i: (i, 0)),
        out_shape=jax.ShapeDtypeStruct((n, d), F32),
        compiler_params=_params(("parallel",)),
        name=name,
    )(h, g.reshape(1, d))


def _sgu_kernel(uv_ref, lng_ref, lnb_ref, wcat_ref, bias_ref, o_ref):
    uv = uv_ref[...]
    u = _gelu(uv[:, :SGU_WIDTH])
    v = _gelu(uv[:, SGU_WIDTH:])
    mu = jnp.mean(v, axis=-1, keepdims=True)
    vc = v - mu
    var = jnp.mean(vc * vc, axis=-1, keepdims=True)
    v = (vc * lax.rsqrt(var + EPS) * lng_ref[...] + lnb_ref[...]).astype(BF16)

    kdim = SGU_GROUPS * SGU_BLOCK
    row_g = lax.broadcasted_iota(jnp.int32, (kdim, SGU_WIDTH), 0) // SGU_BLOCK
    col_g = lax.broadcasted_iota(jnp.int32, (kdim, SGU_WIDTH), 1) // SGU_GDIM
    vbd = jnp.where(row_g == col_g, jnp.concatenate([v] * SGU_GROUPS, axis=0), jnp.zeros((), BF16))

    t_chunk = lax.broadcasted_iota(jnp.int32, (SGU_BLOCK, kdim), 0) // CHUNK
    s_chunk = (lax.broadcasted_iota(jnp.int32, (SGU_BLOCK, kdim), 1) % SGU_BLOCK) // CHUNK
    w = jnp.where(s_chunk <= t_chunk, wcat_ref[...], 0.0).astype(BF16)
    mixed = _dot(w, vbd) + bias_ref[...]
    o_ref[...] = (u * mixed).astype(o_ref.dtype)


def sgu(uv, ln_g, ln_b, w_s, b_s, name):
    n = uv.shape[0]
    wcat = jnp.transpose(w_s, (1, 0, 2)).reshape(SGU_BLOCK, SGU_GROUPS * SGU_BLOCK)
    bias = jnp.repeat(b_s.T, SGU_GDIM, axis=1)
    return pl.pallas_call(
        _sgu_kernel,
        grid=(n // SGU_BLOCK,),
        in_specs=[pl.BlockSpec((SGU_BLOCK, 2 * SGU_WIDTH), lambda i: (i, 0)),
                  pl.BlockSpec((1, SGU_WIDTH), lambda i: (0, 0)),
                  pl.BlockSpec((1, SGU_WIDTH), lambda i: (0, 0)),
                  pl.BlockSpec(wcat.shape, lambda i: (0, 0)),
                  pl.BlockSpec(bias.shape, lambda i: (0, 0))],
        out_specs=pl.BlockSpec((SGU_BLOCK, SGU_WIDTH), lambda i: (i, 0)),
        out_shape=jax.ShapeDtypeStruct((n, SGU_WIDTH), BF16),
        compiler_params=_params(("parallel",)),
        name=name,
    )(uv, ln_g.reshape(1, -1), ln_b.reshape(1, -1), wcat, bias)


GDN_T = 256
HALO = 8


def _bdot(a, b):
    return lax.dot_general(a, b, (((2,), (1,)), ((0,), (0,))), preferred_element_type=F32)


def _bdot_nt(a, b):
    return lax.dot_general(a, b, (((2,), (2,)), ((0,), (0,))), preferred_element_type=F32)


def _bdot_tn(a, b):
    return lax.dot_general(a, b, (((1,), (1,)), ((0,), (0,))), preferred_element_type=F32)


def _split3(x):
    hi = x.astype(BF16)
    r1 = x - hi.astype(F32)
    mid = r1.astype(BF16)
    lo = (r1 - mid.astype(F32)).astype(BF16)
    return hi, mid, lo


def _gdn_kernel(qkv_ref, gate_ref, ba_ref, convw_ref, arow_ref, onorm_ref, o_ref, state_ref, xbuf_ref):
    tb = pl.program_id(1)
    nc = GDN_T // CHUNK
    nu = nc * GDN_HEADS

    @pl.when(tb == 0)
    def _():
        state_ref[...] = jnp.zeros_like(state_ref)
        xbuf_ref[0:HALO, :] = jnp.zeros((HALO, CONV_CH), F32)

    @pl.when(tb != 0)
    def _():
        xbuf_ref[0:HALO, :] = xbuf_ref[GDN_T:GDN_T + HALO, :]

    xbuf_ref[HALO:HALO + GDN_T, :] = qkv_ref[...]
    cw = convw_ref[...]
    y = cw[0:1, :] * xbuf_ref[pl.ds(HALO - 3, GDN_T), :]
    for i in range(1, GDN_CONV):
        y = y + cw[i:i + 1, :] * xbuf_ref[pl.ds(HALO - 3 + i, GDN_T), :]
    qkv = y * _sigmoid(y)

    arow = arow_ref[...]
    ba = ba_ref[...]
    beta_full = _sigmoid(ba)
    g_full = arow[0:1, :] * _softplus(ba + arow[1:2, :])

    ii = lax.broadcasted_iota(jnp.int32, (CHUNK, CHUNK), 0)
    jj = lax.broadcasted_iota(jnp.int32, (CHUNK, CHUNK), 1)
    causal = ii >= jj
    strict = ii > jj

    tril = jnp.broadcast_to(jnp.where(causal, 1.0, 0.0).astype(BF16), (nc, CHUNK, CHUNK))
    gc_full = sum(_bdot(tril, part) for part in _split3(g_full.reshape(nc, CHUNK, LANES)))

    def units(fn):
        return jnp.stack([fn(c, h) for c in range(nc) for h in range(GDN_HEADS)], axis=0)

    def rows(c):
        return slice(c * CHUNK, (c + 1) * CHUNK)

    def l2n(x):
        return x * lax.rsqrt(jnp.sum(x * x, axis=-1, keepdims=True) + EPS)

    q = l2n(units(lambda c, h: qkv[rows(c), h * GDN_DK:(h + 1) * GDN_DK])) * (GDN_DK ** -0.5)
    k = l2n(units(lambda c, h: qkv[rows(c), GDN_QK + h * GDN_DK:GDN_QK + (h + 1) * GDN_DK]))
    v = units(lambda c, h: qkv[rows(c), 2 * GDN_QK + h * GDN_DV:2 * GDN_QK + (h + 1) * GDN_DV])
    beta = units(lambda c, h: beta_full[rows(c), h:h + 1])
    gc = units(lambda c, h: gc_full[c, :, 4 + h:5 + h])

    pick0 = jnp.broadcast_to(jnp.where(jj == 0, 1.0, 0.0).astype(BF16), (nu, CHUNK, CHUNK))
    gc_b = jnp.broadcast_to(gc, (nu, CHUNK, CHUNK))
    gc_row = sum(_bdot_nt(pick0, part) for part in _split3(gc_b))
    gamma = jnp.where(causal, jnp.exp(jnp.where(causal, gc - gc_row, 0.0)), 0.0)
    g_last = gc[:, CHUNK - 1:CHUNK, :]
    egc = jnp.exp(gc)

    kb = k * beta
    kbf = k.astype(BF16)
    lmat = jnp.where(strict, _bdot_nt(kb.astype(BF16), kbf) * gamma, 0.0)
    attn = (_bdot_nt(q.astype(BF16), kbf) * gamma).astype(BF16)
    x = jnp.concatenate([v * beta, kb * egc], axis=2)
    npow = -lmat
    for level in range(6):
        nb = npow.astype(BF16)
        x = x + _bdot(nb, x.astype(BF16))
        if level < 5:
            npow = _bdot(nb, nb)
    u = x[:, :, :GDN_DV].astype(BF16)
    w = x[:, :, GDN_DV:].astype(BF16)
    k_dec = (k * jnp.exp(g_last - gc)).astype(BF16)
    q_eff = (q * egc - _bdot(attn, w)).astype(BF16)
    o_loc = _bdot(attn, u)
    s_mix = _bdot_tn(k_dec, w).astype(BF16)
    s_add = _bdot_tn(k_dec, u)
    decay = jnp.exp(g_last)

    onorm = onorm_ref[...]
    gate = gate_ref[...]
    state = state_ref[...]
    for c in range(nc):
        sl = slice(c * GDN_HEADS, (c + 1) * GDN_HEADS)
        sb = state.astype(BF16)
        o = _bdot(q_eff[sl], sb) + o_loc[sl]
        state = decay[sl] * state - _bdot(s_mix[sl], sb) + s_add[sl]
        o = o * lax.rsqrt(jnp.mean(o * o, axis=-1, keepdims=True) + EPS) * onorm
        for h in range(GDN_HEADS):
            gt = gate[rows(c), h * GDN_DV:(h + 1) * GDN_DV]
            o_ref[rows(c), h * GDN_DV:(h + 1) * GDN_DV] = (o[h] * (gt * _sigmoid(gt))).astype(o_ref.dtype)
    state_ref[...] = state


def gdn(qkv, gate, ba, conv_w, a_log, dt_bias, o_norm, batch, name):
    n = qkv.shape[0]
    s = n // batch
    nt = s // GDN_T
    arow = jnp.zeros((8, LANES), F32).at[0, 4:8].set(-jnp.exp(a_log)).at[1, 4:8].set(dt_bias)
    return pl.pallas_call(
        _gdn_kernel,
        grid=(batch, nt),
        in_specs=[pl.BlockSpec((GDN_T, CONV_CH), lambda b, t: (b * nt + t, 0)),
                  pl.BlockSpec((GDN_T, GDN_V), lambda b, t: (b * nt + t, 0)),
                  pl.BlockSpec((GDN_T, LANES), lambda b, t: (b * nt + t, 0)),
                  pl.BlockSpec((GDN_CONV, CONV_CH), lambda b, t: (0, 0)),
                  pl.BlockSpec((8, LANES), lambda b, t: (0, 0)),
                  pl.BlockSpec((1, GDN_DV), lambda b, t: (0, 0))],
        out_specs=pl.BlockSpec((GDN_T, GDN_V), lambda b, t: (b * nt + t, 0)),
        out_shape=jax.ShapeDtypeStruct((n, GDN_V), BF16),
        scratch_shapes=[pltpu.VMEM((GDN_HEADS, GDN_DK, GDN_DV), F32),
                        pltpu.VMEM((HALO + GDN_T, CONV_CH), F32)],
        compiler_params=_params(("parallel", "arbitrary")),
        name=name,
    )(qkv, gate, ba, conv_w, arow, o_norm.reshape(1, -1))


SB_T = 256
SB_PAIRS = 2


def _sb_kernel(q_ref, k_ref, v_ref, fk_ref, o_ref, qh_ref, z_ref, sp_ref, acc_ref, cs_ref):
    qi = pl.program_id(2)
    lane = lax.broadcasted_iota(jnp.int32, (1, LANES), 1)
    heads = [(p, hh) for p in range(SB_PAIRS) for hh in range(2)]

    for idx, (p, hh) in enumerate(heads):
        q = q_ref[:, p * LANES:(p + 1) * LANES]
        in_head = (lane >= hh * SB_DIM) & (lane < (hh + 1) * SB_DIM)
        qh_ref[idx] = jnp.where(in_head, q, jnp.zeros((), BF16)) * jnp.asarray(SB_DIM ** -0.5, BF16)
    acc_ref[...] = jnp.zeros_like(acc_ref)
    cs_ref[...] = jnp.zeros_like(cs_ref)

    def earlier():
        row = lax.broadcasted_iota(jnp.int32, (SB_T, SB_T), 0)
        col = lax.broadcasted_iota(jnp.int32, (SB_T, SB_T), 1)
        return col < row

    def stage_a(kb, slot, diagonal=False):
        start = pl.multiple_of(kb * SB_T, SB_T)
        for idx, (p, hh) in enumerate(heads):
            kblk = k_ref[pl.ds(start, SB_T), p * LANES:(p + 1) * LANES]
            z = _dot_nt(qh_ref[idx], kblk)
            sp = jnp.maximum(z, 0.0) + jnp.log(1.0 + jnp.exp(-jnp.abs(z)))
            if diagonal:
                sp = jnp.where(earlier(), sp, 0.0)
            z_ref[slot, idx] = z
            sp_ref[slot, idx] = sp.astype(BF16)

    def stage_b(kb, slot, diagonal=False):
        start = pl.multiple_of(kb * SB_T, SB_T)
        for idx, (p, hh) in enumerate(heads):
            vblk = v_ref[pl.ds(start, SB_T), p * LANES:(p + 1) * LANES]
            ssum = _dot(sp_ref[slot, idx], fk_ref[...])
            att = jnp.exp(z_ref[slot, idx] - ssum - cs_ref[idx])
            if diagonal:
                att = jnp.where(earlier(), att, 0.0)
            acc_ref[idx] += _dot(att.astype(BF16), vblk)
            cs_ref[idx] += ssum[:, 0:1]

    stage_a(qi, 0, True)
    stage_b(qi, 0, True)

    def step(t, slot):
        stage_a(qi - 1 - t, slot)
        stage_b(qi - t, 1 - slot)

    @pl.when(qi > 0)
    def _():
        stage_a(qi - 1, 0)

        def two_steps(m, carry):
            step(2 * m + 1, 1)
            step(2 * m + 2, 0)
            return carry

        lax.fori_loop(0, (qi - 1) // 2, two_steps, 0)

        @pl.when(qi % 2 == 0)
        def _():
            step(qi - 1, 1)
            stage_b(0, 1)

        @pl.when(qi % 2 == 1)
        def _():
            stage_b(0, 0)

    for p in range(SB_PAIRS):
        o_ref[:, p * LANES:(p + 1) * LANES] = jnp.where(
            lane < SB_DIM, acc_ref[2 * p], acc_ref[2 * p + 1]).astype(o_ref.dtype)


def stick_breaking(q, k, v, batch, name):
    n, d = q.shape
    s = n // batch
    width = SB_PAIRS * LANES
    nh = 2 * SB_PAIRS
    q3, k3, v3 = (t.reshape(batch, s, d) for t in (q, k, v))
    from_key = jnp.tril(jnp.ones((SB_T, SB_T), BF16))
    out = pl.pallas_call(
        _sb_kernel,
        grid=(batch, d // width, s // SB_T),
        in_specs=[pl.BlockSpec((None, SB_T, width), lambda b, p, i: (b, i, p)),
                  pl.BlockSpec((None, s, width), lambda b, p, i: (b, 0, p)),
                  pl.BlockSpec((None, s, width), lambda b, p, i: (b, 0, p)),
                  pl.BlockSpec((SB_T, SB_T), lambda b, p, i: (0, 0))],
        out_specs=pl.BlockSpec((None, SB_T, width), lambda b, p, i: (b, i, p)),
        out_shape=jax.ShapeDtypeStruct((batch, s, d), BF16),
        scratch_shapes=[pltpu.VMEM((nh, SB_T, LANES), BF16),
                        pltpu.VMEM((2, nh, SB_T, SB_T), F32),
                        pltpu.VMEM((2, nh, SB_T, SB_T), BF16),
                        pltpu.VMEM((nh, SB_T, LANES), F32),
                        pltpu.VMEM((nh, SB_T, 1), F32)],
        compiler_params=_params(("parallel", "parallel", "arbitrary")),
        name=name,
    )(q3, k3, v3, from_key)
    return out.reshape(n, d)


def kernel(x, e_norm1, e_w_in, e_sgu_ln_g, e_sgu_ln_b, e_sgu_w, e_sgu_b, e_conv_w, e_a_log, e_dt_bias, e_o_norm, e_w_out, e_norm2, e_ffn_w1, e_ffn_w3, e_ffn_w2, o_norm1, o_w_qkv, o_w_out, o_norm2, o_router, o_moe_w1, o_moe_w3, o_moe_w2, final_norm):
    batch, seq, d = x.shape
    n = batch * seq
    h = x.reshape(n, d)
    n_main = 2 * SGU_WIDTH + CONV_CH + GDN_V
    depth = e_norm1.shape[0] + o_norm1.shape[0]
    for layer in range(depth):
        i = layer // 2
        if layer % 2 == 0:
            w_in = e_w_in[i]
            w_in = jnp.concatenate(
                [w_in, jnp.zeros((d, LANES - (w_in.shape[1] - n_main)), F32)], axis=1).astype(BF16)
            splits = (0, 2 * SGU_WIDTH, 2 * SGU_WIDTH + CONV_CH, n_main, n_main + LANES)
            uv, qkv, gate, ba = norm_proj(h, e_norm1[i], w_in, splits, (F32, F32, F32, F32), 512,
                                          f"even{i}_in_proj")
            y_a = sgu(uv, e_sgu_ln_g[i], e_sgu_ln_b[i], e_sgu_w[i], e_sgu_b[i], f"even{i}_sgu")
            y_b = gdn(qkv, gate, ba, e_conv_w[i], e_a_log[i], e_dt_bias[i], e_o_norm[i], batch,
                      f"even{i}_gdn")
            h = proj_residual(h, e_w_out[i].astype(BF16), [y_a, y_b], 1024, f"even{i}_out_proj")
            h = ffn_residual(h, e_norm2[i], e_ffn_w1[i].astype(BF16), e_ffn_w3[i].astype(BF16),
                             e_ffn_w2[i].astype(BF16), 1024, 256, f"even{i}_ffn")
        else:
            q, k, v = norm_proj(h, o_norm1[i], o_w_qkv[i].astype(BF16), (0, d, 2 * d, 3 * d),
                                (BF16, BF16, BF16), 512, f"odd{i}_qkv_proj")
            att = stick_breaking(q, k, v, batch, f"odd{i}_stick_breaking")
            h = proj_residual(h, o_w_out[i].astype(BF16), [att], 1024, f"odd{i}_out_proj")
            last = layer == depth - 1
            h = moe_residual(h, o_norm2[i], o_router[i], o_moe_w1[i].astype(BF16), o_moe_w3[i].astype(BF16),
                             o_moe_w2[i].astype(BF16), final_norm if last else None, f"odd{i}_moe")
    if depth % 2 == 1:
        h = final_norm_call(h, final_norm, 1024, "final_norm")
    return h.reshape(batch, seq, d)
```

```python
import functools

import jax
import jax.numpy as jnp
from jax import lax
from jax.experimental import pallas as pl
from jax.experimental.pallas import tpu as pltpu

F32 = jnp.float32
BF16 = jnp.bfloat16

D_MODEL = 1024
EPS = 1e-6
CHUNK = 64
SGU_BLOCK = 128
SGU_GROUPS = 8
SGU_WIDTH = 512
SGU_GDIM = 64
GDN_HEADS = 4
GDN_DK = 128
GDN_DV = 128
GDN_CONV = 4
GDN_QK = 512
GDN_V = 512
CONV_CH = 1536
SB_HEADS = 16
SB_DIM = 64
D_FF = 2816
N_EXPERTS = 8
LANES = 128
VMEM_LIMIT = 48 * 1024 * 1024


def _params(sem):
    return pltpu.CompilerParams(dimension_semantics=sem, vmem_limit_bytes=VMEM_LIMIT)


def _rms(x, g):
    return x * lax.rsqrt(jnp.mean(x * x, axis=-1, keepdims=True) + EPS) * g


def _sigmoid(x):
    return 1.0 / (1.0 + jnp.exp(-x))


def _softplus(x):
    return jnp.maximum(x, 0.0) + jnp.log1p(jnp.exp(-jnp.abs(x)))


def _gelu(x):
    return 0.5 * x * (1.0 + jnp.tanh(0.7978845608028654 * (x + 0.044715 * (x * x * x))))


def _dot(a, b):
    return jnp.dot(a, b, preferred_element_type=F32)


def _dot_nt(a, b):
    return lax.dot_general(a, b, (((1,), (1,)), ((), ())), preferred_element_type=F32)


def _dot_tn(a, b):
    return lax.dot_general(a, b, (((0,), (0,)), ((), ())), preferred_element_type=F32)


def _norm_proj_kernel(h_ref, g_ref, w_ref, *o_refs, splits):
    xn = _rms(h_ref[...], g_ref[...]).astype(BF16)
    for o_ref, a, b in zip(o_refs, splits[:-1], splits[1:]):
        o_ref[...] = _dot(xn, w_ref[:, a:b]).astype(o_ref.dtype)


def norm_proj(h, g, w, splits, dtypes, tm, name):
    n, d = h.shape
    outs = [jax.ShapeDtypeStruct((n, b - a), dt) for a, b, dt in zip(splits[:-1], splits[1:], dtypes)]
    return pl.pallas_call(
        functools.partial(_norm_proj_kernel, splits=splits),
        grid=(n // tm,),
        in_specs=[pl.BlockSpec((tm, d), lambda i: (i, 0)),
                  pl.BlockSpec((1, d), lambda i: (0, 0)),
                  pl.BlockSpec(w.shape, lambda i: (0, 0))],
        out_specs=[pl.BlockSpec((tm, o.shape[1]), lambda i: (i, 0)) for o in outs],
        out_shape=outs,
        compiler_params=_params(("parallel",)),
        name=name,
    )(h, g.reshape(1, d), w)


def _proj_res_kernel(h_ref, w_ref, *refs, splits):
    x_refs, o_ref = refs[:-1], refs[-1]
    acc = h_ref[...]
    for x_ref, a, b in zip(x_refs, splits[:-1], splits[1:]):
        acc = acc + _dot(x_ref[...], w_ref[a:b, :])
    o_ref[...] = acc


def proj_residual(h, w, xs, tm, name):
    n, d = h.shape
    splits = [0]
    for x in xs:
        splits.append(splits[-1] + x.shape[1])
    return pl.pallas_call(
        functools.partial(_proj_res_kernel, splits=tuple(splits)),
        grid=(n // tm,),
        in_specs=[pl.BlockSpec((tm, d), lambda i: (i, 0)),
                  pl.BlockSpec(w.shape, lambda i: (0, 0))]
                 + [pl.BlockSpec((tm, x.shape[1]), lambda i: (i, 0)) for x in xs],
        out_specs=pl.BlockSpec((tm, d), lambda i: (i, 0)),
        out_shape=jax.ShapeDtypeStruct((n, d), F32),
        compiler_params=_params(("parallel",)),
        name=name,
    )(h, w, *xs)


def _ffn_kernel(h_ref, g_ref, w1_ref, w3_ref, w2_ref, o_ref, xn_ref, acc_ref):
    j = pl.program_id(1)

    @pl.when(j == 0)
    def _():
        xn_ref[...] = _rms(h_ref[...], g_ref[...]).astype(BF16)
        acc_ref[...] = jnp.zeros_like(acc_ref)

    xn = xn_ref[...]
    a = _dot(xn, w1_ref[...])
    b = _dot(xn, w3_ref[...])
    hidden = (a * _sigmoid(a) * b).astype(BF16)
    acc_ref[...] += _dot(hidden, w2_ref[...])

    @pl.when(j == pl.num_programs(1) - 1)
    def _():
        o_ref[...] = h_ref[...] + acc_ref[...]


def ffn_residual(h, g, w1, w3, w2, tm, tf, name):
    n, d = h.shape
    f = w1.shape[1]
    return pl.pallas_call(
        _ffn_kernel,
        grid=(n // tm, f // tf),
        in_specs=[pl.BlockSpec((tm, d), lambda i, j: (i, 0)),
                  pl.BlockSpec((1, d), lambda i, j: (0, 0)),
                  pl.BlockSpec((d, tf), lambda i, j: (0, j)),
                  pl.BlockSpec((d, tf), lambda i, j: (0, j)),
                  pl.BlockSpec((tf, d), lambda i, j: (j, 0))],
        out_specs=pl.BlockSpec((tm, d), lambda i, j: (i, 0)),
        out_shape=jax.ShapeDtypeStruct((n, d), F32),
        scratch_shapes=[pltpu.VMEM((tm, d), BF16), pltpu.VMEM((tm, d), F32)],
        compiler_params=_params(("parallel", "arbitrary")),
        name=name,
    )(h, g.reshape(1, d), w1, w3, w2)


MOE_TM = 512
ROUTER_TM = 256
META_E1, META_E2, META_R1, META_R2, META_G1, META_G2 = range(6)


def _router_kernel(h_ref, g_ref, wr_ref, meta_ref, cnt_ref, run_ref):
    @pl.when(pl.program_id(0) == 0)
    def _():
        run_ref[...] = jnp.zeros_like(run_ref)

    xn = _rms(h_ref[...], g_ref[...])
    logits = jnp.dot(xn, wr_ref[...], preferred_element_type=F32,
                     precision=lax.Precision.HIGHEST)
    tm = logits.shape[0]
    lane = lax.broadcasted_iota(jnp.int32, logits.shape, 1)
    neg = jnp.float32(-jnp.inf)
    l1 = jnp.where(lane < N_EXPERTS, logits, neg)
    m1 = jnp.max(l1, axis=-1, keepdims=True)
    i1 = jnp.min(jnp.where(l1 == m1, lane, LANES), axis=-1, keepdims=True)
    l2 = jnp.where(lane == i1, neg, l1)
    m2 = jnp.max(l2, axis=-1, keepdims=True)
    i2 = jnp.min(jnp.where(l2 == m2, lane, LANES), axis=-1, keepdims=True)
    e2 = jnp.exp(m2 - m1)
    g1 = 1.0 / (1.0 + e2)
    g2 = e2 / (1.0 + e2)

    chosen = (lane == i1) | (lane == i2)
    onehot = jnp.where(chosen, 1.0, 0.0)
    rr = lax.broadcasted_iota(jnp.int32, (tm, tm), 0)
    cc = lax.broadcasted_iota(jnp.int32, (tm, tm), 1)
    before = jnp.where(cc < rr, 1.0, 0.0).astype(BF16)
    prefix = _dot(before, onehot.astype(BF16)) + run_ref[0:1, :]
    r1 = jnp.sum(jnp.where(lane == i1, prefix, 0.0), axis=-1, keepdims=True)
    r2 = jnp.sum(jnp.where(lane == i2, prefix, 0.0), axis=-1, keepdims=True)
    run_ref[...] = run_ref[...] + jnp.sum(onehot, axis=0, keepdims=True)
    cnt_ref[...] = run_ref[...]

    meta = jnp.zeros(logits.shape, F32)
    for idx, val in ((META_E1, i1.astype(F32)), (META_E2, i2.astype(F32)), (META_R1, r1), (META_R2, r2),
                     (META_G1, g1), (META_G2, g2)):
        meta = jnp.where(lane == idx, val, meta)
    meta_ref[...] = meta


def router(h, g, wr_pad, name):
    n, d = h.shape
    tm = ROUTER_TM
    return pl.pallas_call(
        _router_kernel,
        grid=(n // tm,),
        in_specs=[pl.BlockSpec((tm, d), lambda i: (i, 0)),
                  pl.BlockSpec((1, d), lambda i: (0, 0)),
                  pl.BlockSpec((d, LANES), lambda i: (0, 0))],
        out_specs=[pl.BlockSpec((tm, LANES), lambda i: (i, 0)),
                   pl.BlockSpec((8, LANES), lambda i: (0, 0))],
        out_shape=[jax.ShapeDtypeStruct((n, LANES), F32), jax.ShapeDtypeStruct((8, LANES), F32)],
        scratch_shapes=[pltpu.VMEM((8, LANES), F32)],
        compiler_params=_params(("arbitrary",)),
        name=name,
    )(h, g.reshape(1, d), wr_pad)


def _dispatch_kernel(pos1_ref, pos2_ref, h_ref, xs_init_hbm, xs_hbm, sem, *, tm):
    del xs_init_hbm

    def copies(r):
        src = h_ref.at[pl.ds(r, 1)]
        return (pltpu.make_async_copy(src, xs_hbm.at[pl.ds(pos1_ref[r], 1)], sem.at[0]),
                pltpu.make_async_copy(src, xs_hbm.at[pl.ds(pos2_ref[r], 1)], sem.at[1]))

    def start(r, carry):
        for cp in copies(r):
            cp.start()
        return carry

    def wait(r, carry):
        for cp in copies(r):
            cp.wait()
        return carry

    lax.fori_loop(0, tm, start, 0)
    lax.fori_loop(0, tm, wait, 0)


def dispatch(h, pos1, pos2, n_rows, tm, name):
    n, d = h.shape
    return pl.pallas_call(
        functools.partial(_dispatch_kernel, tm=tm),
        grid=(n // tm,),
        in_specs=[pl.BlockSpec((tm,), lambda i: (i,), memory_space=pltpu.SMEM),
                  pl.BlockSpec((tm,), lambda i: (i,), memory_space=pltpu.SMEM),
                  pl.BlockSpec((tm, d), lambda i: (i, 0)),
                  pl.BlockSpec(memory_space=pl.ANY)],
        out_specs=pl.BlockSpec(memory_space=pl.ANY),
        out_shape=jax.ShapeDtypeStruct((n_rows, d), F32),
        scratch_shapes=[pltpu.SemaphoreType.DMA((2,))],
        input_output_aliases={3: 0},
        compiler_params=_params(("arbitrary",)),
        name=name,
    )(pos1, pos2, h, jnp.zeros((n_rows, d), F32))


def _experts_kernel(te_ref, nt_ref, xs_ref, g_ref, w1_ref, w3_ref, w2_ref, ys_ref, xn_ref, acc_ref):
    del te_ref
    i = pl.program_id(0)
    j = pl.program_id(1)

    @pl.when(i < nt_ref[0])
    def _():
        @pl.when(j == 0)
        def _():
            xn_ref[...] = _rms(xs_ref[...], g_ref[...]).astype(BF16)
            acc_ref[...] = jnp.zeros_like(acc_ref)

        xn = xn_ref[...]
        a = _dot(xn, w1_ref[...])
        b = _dot(xn, w3_ref[...])
        hidden = (a * _sigmoid(a) * b).astype(BF16)
        acc_ref[...] += _dot(hidden, w2_ref[...])

        @pl.when(j == pl.num_programs(1) - 1)
        def _():
            ys_ref[...] = acc_ref[...]

    @pl.when((i >= nt_ref[0]) & (j == pl.num_programs(1) - 1))
    def _():
        ys_ref[...] = jnp.zeros_like(ys_ref)


def experts(xs, g, tile_expert, n_tiles, w1, w3, w2, tf, name):
    n_rows, d = xs.shape
    f = w1.shape[2]
    tm = MOE_TM
    nj = f // tf

    def used(i, nt):
        return jnp.maximum(jnp.minimum(i, nt[0] - 1), 0)

    def row_map(i, j, te, nt):
        return (used(i, nt), 0)

    def up_map(i, j, te, nt):
        return (te[used(i, nt)], 0, jnp.where(i < nt[0], j, nj - 1))

    def down_map(i, j, te, nt):
        return (te[used(i, nt)], jnp.where(i < nt[0], j, nj - 1), 0)

    return pl.pallas_call(
        _experts_kernel,
        grid_spec=pltpu.PrefetchScalarGridSpec(
            num_scalar_prefetch=2,
            grid=(n_rows // tm, nj),
            in_specs=[pl.BlockSpec((tm, d), row_map),
                      pl.BlockSpec((1, d), lambda i, j, te, nt: (0, 0)),
                      pl.BlockSpec((None, d, tf), up_map),
                      pl.BlockSpec((None, d, tf), up_map),
                      pl.BlockSpec((None, tf, d), down_map)],
            out_specs=pl.BlockSpec((tm, d), lambda i, j, te, nt: (i, 0)),
            scratch_shapes=[pltpu.VMEM((tm, d), BF16), pltpu.VMEM((tm, d), F32)]),
        out_shape=jax.ShapeDtypeStruct((n_rows, d), F32),
        compiler_params=_params(("arbitrary", "arbitrary")),
        name=name,
    )(tile_expert, n_tiles, xs, g.reshape(1, d), w1, w3, w2)


def _combine_kernel(pos1_ref, pos2_ref, meta_ref, h_ref, gf_ref, ys_hbm, o_ref, buf_ref, sem, *, tm, final):
    def copies(r):
        return (pltpu.make_async_copy(ys_hbm.at[pl.ds(pos1_ref[r], 1)], buf_ref.at[0, pl.ds(r, 1)], sem.at[0]),
                pltpu.make_async_copy(ys_hbm.at[pl.ds(pos2_ref[r], 1)], buf_ref.at[1, pl.ds(r, 1)], sem.at[1]))

    def start(r, carry):
        for cp in copies(r):
            cp.start()
        return carry

    def wait(r, carry):
        for cp in copies(r):
            cp.wait()
        return carry

    lax.fori_loop(0, tm, start, 0)
    lax.fori_loop(0, tm, wait, 0)
    meta = meta_ref[...]
    out = (h_ref[...] + meta[:, META_G1:META_G1 + 1] * buf_ref[0]
           + meta[:, META_G2:META_G2 + 1] * buf_ref[1])
    if final:
        out = _rms(out, gf_ref[...])
    o_ref[...] = out


def combine(h, meta, pos1, pos2, ys, final_g, tm, name):
    n, d = h.shape
    final = final_g is not None
    gf = (final_g if final else jnp.ones((d,), F32)).reshape(1, d)
    return pl.pallas_call(
        functools.partial(_combine_kernel, tm=tm, final=final),
        grid=(n // tm,),
        in_specs=[pl.BlockSpec((tm,), lambda i: (i,), memory_space=pltpu.SMEM),
                  pl.BlockSpec((tm,), lambda i: (i,), memory_space=pltpu.SMEM),
                  pl.BlockSpec((tm, LANES), lambda i: (i, 0)),
                  pl.BlockSpec((tm, d), lambda i: (i, 0)),
                  pl.BlockSpec((1, d), lambda i: (0, 0)),
                  pl.BlockSpec(memory_space=pl.ANY)],
        out_specs=pl.BlockSpec((tm, d), lambda i: (i, 0)),
        out_shape=jax.ShapeDtypeStruct((n, d), F32),
        scratch_shapes=[pltpu.VMEM((2, tm, d), F32), pltpu.SemaphoreType.DMA((2,))],
        compiler_params=_params(("arbitrary",)),
        name=name,
    )(pos1, pos2, meta, h, gf, ys)


def moe_residual(h, g, w_router, w1, w3, w2, final_g, name):
    n, d = h.shape
    wr = jnp.concatenate([w_router, jnp.zeros((d, LANES - N_EXPERTS), F32)], axis=1)
    meta, cnt = router(h, g, wr, f"{name}_router")
    cnt = cnt[0, :N_EXPERTS].astype(jnp.int32)
    n_tile_e = (cnt + MOE_TM - 1) // MOE_TM
    tile_end = jnp.cumsum(n_tile_e)
    offset = (tile_end - n_tile_e) * MOE_TM
    e1 = meta[:, META_E1].astype(jnp.int32)
    e2 = meta[:, META_E2].astype(jnp.int32)
    pos1 = offset[e1] + meta[:, META_R1].astype(jnp.int32)
    pos2 = offset[e2] + meta[:, META_R2].astype(jnp.int32)
    max_tiles = (2 * n) // MOE_TM + N_EXPERTS
    tile_ids = jnp.arange(max_tiles, dtype=jnp.int32)
    tile_expert = jnp.minimum(jnp.sum(tile_ids[:, None] >= tile_end[None, :], axis=1), N_EXPERTS - 1)
    n_tiles = tile_end[-1:].astype(jnp.int32)
    xs = dispatch(h, pos1, pos2, max_tiles * MOE_TM, 1024, f"{name}_dispatch")
    ys = experts(xs, g, tile_expert.astype(jnp.int32), n_tiles, w1, w3, w2, 1408, f"{name}_experts")
    return combine(h, meta, pos1, pos2, ys, final_g, 512, f"{name}_combine")


def _final_norm_kernel(h_ref, g_ref, o_ref):
    o_ref[...] = _rms(h_ref[...], g_ref[...])


def final_norm_call(h, g, tm, name):
    n, d = h.shape
    return pl.pallas_call(
        _final_norm_kernel,
        grid=(n // tm,),
        in_specs=[pl.BlockSpec((tm, d), lambda i: (i, 0)),
                  pl.BlockSpec((1, d), lambda i: (0, 0))],
        out_specs=pl.BlockSpec((tm, d), lambda i: (i, 0)),
        out_shape=jax.ShapeDtypeStruct((n, d), F32),
        compiler_params=_params(("parallel",)),
        name=name,
    )(h, g.reshape(1, d))


def _sgu_kernel(uv_ref, lng_ref, lnb_ref, wcat_ref, bias_ref, o_ref):
    uv = uv_ref[...]
    u = _gelu(uv[:, :SGU_WIDTH])
    v = _gelu(uv[:, SGU_WIDTH:])
    mu = jnp.mean(v, axis=-1, keepdims=True)
    vc = v - mu
    var = jnp.mean(vc * vc, axis=-1, keepdims=True)
    v = (vc * lax.rsqrt(var + EPS) * lng_ref[...] + lnb_ref[...]).astype(BF16)

    kdim = SGU_GROUPS * SGU_BLOCK
    row_g = lax.broadcasted_iota(jnp.int32, (kdim, SGU_WIDTH), 0) // SGU_BLOCK
    col_g = lax.broadcasted_iota(jnp.int32, (kdim, SGU_WIDTH), 1) // SGU_GDIM
    vbd = jnp.where(row_g == col_g, jnp.concatenate([v] * SGU_GROUPS, axis=0), jnp.zeros((), BF16))

    t_chunk = lax.broadcasted_iota(jnp.int32, (SGU_BLOCK, kdim), 0) // CHUNK
    s_chunk = (lax.broadcasted_iota(jnp.int32, (SGU_BLOCK, kdim), 1) % SGU_BLOCK) // CHUNK
    w = jnp.where(s_chunk <= t_chunk, wcat_ref[...], 0.0).astype(BF16)
    mixed = _dot(w, vbd) + bias_ref[...]
    o_ref[...] = (u * mixed).astype(o_ref.dtype)


def sgu(uv, ln_g, ln_b, w_s, b_s, name):
    n = uv.shape[0]
    wcat = jnp.transpose(w_s, (1, 0, 2)).reshape(SGU_BLOCK, SGU_GROUPS * SGU_BLOCK)
    bias = jnp.repeat(b_s.T, SGU_GDIM, axis=1)
    return pl.pallas_call(
        _sgu_kernel,
        grid=(n // SGU_BLOCK,),
        in_specs=[pl.BlockSpec((SGU_BLOCK, 2 * SGU_WIDTH), lambda i: (i, 0)),
                  pl.BlockSpec((1, SGU_WIDTH), lambda i: (0, 0)),
                  pl.BlockSpec((1, SGU_WIDTH), lambda i: (0, 0)),
                  pl.BlockSpec(wcat.shape, lambda i: (0, 0)),
                  pl.BlockSpec(bias.shape, lambda i: (0, 0))],
        out_specs=pl.BlockSpec((SGU_BLOCK, SGU_WIDTH), lambda i: (i, 0)),
        out_shape=jax.ShapeDtypeStruct((n, SGU_WIDTH), BF16),
        compiler_params=_params(("parallel",)),
        name=name,
    )(uv, ln_g.reshape(1, -1), ln_b.reshape(1, -1), wcat, bias)


GDN_T = 256
HALO = 8


def _bdot(a, b):
    return lax.dot_general(a, b, (((2,), (1,)), ((0,), (0,))), preferred_element_type=F32)


def _bdot_nt(a, b):
    return lax.dot_general(a, b, (((2,), (2,)), ((0,), (0,))), preferred_element_type=F32)


def _bdot_tn(a, b):
    return lax.dot_general(a, b, (((1,), (1,)), ((0,), (0,))), preferred_element_type=F32)


def _split3(x):
    hi = x.astype(BF16)
    r1 = x - hi.astype(F32)
    mid = r1.astype(BF16)
    lo = (r1 - mid.astype(F32)).astype(BF16)
    return hi, mid, lo


def _gdn_kernel(qkv_ref, gate_ref, ba_ref, convw_ref, arow_ref, onorm_ref, o_ref, state_ref, xbuf_ref):
    tb = pl.program_id(1)
    nc = GDN_T // CHUNK
    nu = nc * GDN_HEADS

    @pl.when(tb == 0)
    def _():
        state_ref[...] = jnp.zeros_like(state_ref)
        xbuf_ref[0:HALO, :] = jnp.zeros((HALO, CONV_CH), F32)

    @pl.when(tb != 0)
    def _():
        xbuf_ref[0:HALO, :] = xbuf_ref[GDN_T:GDN_T + HALO, :]

    xbuf_ref[HALO:HALO + GDN_T, :] = qkv_ref[...]
    cw = convw_ref[...]
    y = cw[0:1, :] * xbuf_ref[pl.ds(HALO - 3, GDN_T), :]
    for i in range(1, GDN_CONV):
        y = y + cw[i:i + 1, :] * xbuf_ref[pl.ds(HALO - 3 + i, GDN_T), :]
    qkv = y * _sigmoid(y)

    arow = arow_ref[...]
    ba = ba_ref[...]
    beta_full = _sigmoid(ba)
    g_full = arow[0:1, :] * _softplus(ba + arow[1:2, :])

    ii = lax.broadcasted_iota(jnp.int32, (CHUNK, CHUNK), 0)
    jj = lax.broadcasted_iota(jnp.int32, (CHUNK, CHUNK), 1)
    causal = ii >= jj
    strict = ii > jj

    tril = jnp.broadcast_to(jnp.where(causal, 1.0, 0.0).astype(BF16), (nc, CHUNK, CHUNK))
    gc_full = sum(_bdot(tril, part) for part in _split3(g_full.reshape(nc, CHUNK, LANES)))

    def units(fn):
        return jnp.stack([fn(c, h) for c in range(nc) for h in range(GDN_HEADS)], axis=0)

    def rows(c):
        return slice(c * CHUNK, (c + 1) * CHUNK)

    def l2n(x):
        return x * lax.rsqrt(jnp.sum(x * x, axis=-1, keepdims=True) + EPS)

    q = l2n(units(lambda c, h: qkv[rows(c), h * GDN_DK:(h + 1) * GDN_DK])) * (GDN_DK ** -0.5)
    k = l2n(units(lambda c, h: qkv[rows(c), GDN_QK + h * GDN_DK:GDN_QK + (h + 1) * GDN_DK]))
    v = units(lambda c, h: qkv[rows(c), 2 * GDN_QK + h * GDN_DV:2 * GDN_QK + (h + 1) * GDN_DV])
    beta = units(lambda c, h: beta_full[rows(c), h:h + 1])
    gc = units(lambda c, h: gc_full[c, :, 4 + h:5 + h])

    pick0 = jnp.broadcast_to(jnp.where(jj == 0, 1.0, 0.0).astype(BF16), (nu, CHUNK, CHUNK))
    gc_b = jnp.broadcast_to(gc, (nu, CHUNK, CHUNK))
    gc_row = sum(_bdot_nt(pick0, part) for part in _split3(gc_b))
    gamma = jnp.where(causal, jnp.exp(jnp.where(causal, gc - gc_row, 0.0)), 0.0)
    g_last = gc[:, CHUNK - 1:CHUNK, :]
    egc = jnp.exp(gc)

    kb = k * beta
    kbf = k.astype(BF16)
    lmat = jnp.where(strict, _bdot_nt(kb.astype(BF16), kbf) * gamma, 0.0)
    attn = (_bdot_nt(q.astype(BF16), kbf) * gamma).astype(BF16)
    x = jnp.concatenate([v * beta, kb * egc], axis=2)
    npow = -lmat
    for level in range(6):
        nb = npow.astype(BF16)
        x = x + _bdot(nb, x.astype(BF16))
        if level < 5:
            npow = _bdot(nb, nb)
    u = x[:, :, :GDN_DV].astype(BF16)
    w = x[:, :, GDN_DV:].astype(BF16)
    k_dec = (k * jnp.exp(g_last - gc)).astype(BF16)
    q_eff = (q * egc - _bdot(attn, w)).astype(BF16)
    o_loc = _bdot(attn, u)
    s_mix = _bdot_tn(k_dec, w).astype(BF16)
    s_add = _bdot_tn(k_dec, u)
    decay = jnp.exp(g_last)

    onorm = onorm_ref[...]
    gate = gate_ref[...]
    state = state_ref[...]
    for c in range(nc):
        sl = slice(c * GDN_HEADS, (c + 1) * GDN_HEADS)
        sb = state.astype(BF16)
        o = _bdot(q_eff[sl], sb) + o_loc[sl]
        state = decay[sl] * state - _bdot(s_mix[sl], sb) + s_add[sl]
        o = o * lax.rsqrt(jnp.mean(o * o, axis=-1, keepdims=True) + EPS) * onorm
        for h in range(GDN_HEADS):
            gt = gate[rows(c), h * GDN_DV:(h + 1) * GDN_DV]
            o_ref[rows(c), h * GDN_DV:(h + 1) * GDN_DV] = (o[h] * (gt * _sigmoid(gt))).astype(o_ref.dtype)
    state_ref[...] = state


def gdn(qkv, gate, ba, conv_w, a_log, dt_bias, o_norm, batch, name):
    n = qkv.shape[0]
    s = n // batch
    nt = s // GDN_T
    arow = jnp.zeros((8, LANES), F32).at[0, 4:8].set(-jnp.exp(a_log)).at[1, 4:8].set(dt_bias)
    return pl.pallas_call(
        _gdn_kernel,
        grid=(batch, nt),
        in_specs=[pl.BlockSpec((GDN_T, CONV_CH), lambda b, t: (b * nt + t, 0)),
                  pl.BlockSpec((GDN_T, GDN_V), lambda b, t: (b * nt + t, 0)),
                  pl.BlockSpec((GDN_T, LANES), lambda b, t: (b * nt + t, 0)),
                  pl.BlockSpec((GDN_CONV, CONV_CH), lambda b, t: (0, 0)),
                  pl.BlockSpec((8, LANES), lambda b, t: (0, 0)),
                  pl.BlockSpec((1, GDN_DV), lambda b, t: (0, 0))],
        out_specs=pl.BlockSpec((GDN_T, GDN_V), lambda b, t: (b * nt + t, 0)),
        out_shape=jax.ShapeDtypeStruct((n, GDN_V), BF16),
        scratch_shapes=[pltpu.VMEM((GDN_HEADS, GDN_DK, GDN_DV), F32),
                        pltpu.VMEM((HALO + GDN_T, CONV_CH), F32)],
        compiler_params=_params(("parallel", "arbitrary")),
        name=name,
    )(qkv, gate, ba, conv_w, arow, o_norm.reshape(1, -1))


SB_T = 256
SB_PAIRS = 2
SB_DEAD = 120.0


def _sb_kernel(q_ref, k_ref, v_ref, fk_ref, o_ref, qh_ref, z_ref, sp_ref, acc_ref, cs_ref):
    qi = pl.program_id(2)
    lane = lax.broadcasted_iota(jnp.int32, (1, LANES), 1)
    heads = [(p, hh) for p in range(SB_PAIRS) for hh in range(2)]

    for idx, (p, hh) in enumerate(heads):
        q = q_ref[:, p * LANES:(p + 1) * LANES]
        in_head = (lane >= hh * SB_DIM) & (lane < (hh + 1) * SB_DIM)
        qh_ref[idx] = jnp.where(in_head, q, jnp.zeros((), BF16)) * jnp.asarray(SB_DIM ** -0.5, BF16)
    acc_ref[...] = jnp.zeros_like(acc_ref)
    cs_ref[...] = jnp.zeros_like(cs_ref)

    def earlier():
        row = lax.broadcasted_iota(jnp.int32, (SB_T, SB_T), 0)
        col = lax.broadcasted_iota(jnp.int32, (SB_T, SB_T), 1)
        return col < row

    def stage_a(kb, slot, diagonal=False):
        start = pl.multiple_of(kb * SB_T, SB_T)
        for idx, (p, hh) in enumerate(heads):
            kblk = k_ref[pl.ds(start, SB_T), p * LANES:(p + 1) * LANES]
            z = _dot_nt(qh_ref[idx], kblk)
            sp = jnp.maximum(z, 0.0) + jnp.log(1.0 + jnp.exp(-jnp.abs(z)))
            if diagonal:
                sp = jnp.where(earlier(), sp, 0.0)
            z_ref[slot, idx] = z
            sp_ref[slot, idx] = sp.astype(BF16)

    def stage_b(kb, slot, diagonal=False):
        start = pl.multiple_of(kb * SB_T, SB_T)
        for idx, (p, hh) in enumerate(heads):
            vblk = v_ref[pl.ds(start, SB_T), p * LANES:(p + 1) * LANES]
            ssum = _dot(sp_ref[slot, idx], fk_ref[...])
            att = jnp.exp(jnp.minimum(z_ref[slot, idx] - ssum, 0.0) - cs_ref[idx])
            if diagonal:
                att = jnp.where(earlier(), att, 0.0)
            acc_ref[idx] += _dot(att.astype(BF16), vblk)
            cs_ref[idx] += ssum[:, 0:1]

    def live():
        return (jnp.min(cs_ref[...]) < SB_DEAD).astype(jnp.int32)

    @pl.when(qi == 0)
    def _():
        stage_a(0, 0, True)
        stage_b(0, 0, True)

    @pl.when(qi > 0)
    def _():
        stage_a(qi, 0, True)
        stage_a(qi - 1, 1)
        stage_b(qi, 0, True)
        stage_b(qi - 1, 1)

        def more(carry):
            kb, go = carry
            return (kb >= 0) & (go > 0)

        def one_tile(carry):
            kb, _ = carry
            stage_a(kb, 0)
            stage_b(kb, 0)
            return kb - 1, live()

        lax.while_loop(more, one_tile, (qi - 2, live()))

    for p in range(SB_PAIRS):
        o_ref[:, p * LANES:(p + 1) * LANES] = jnp.where(
            lane < SB_DIM, acc_ref[2 * p], acc_ref[2 * p + 1]).astype(o_ref.dtype)


def stick_breaking(q, k, v, batch, name):
    n, d = q.shape
    s = n // batch
    width = SB_PAIRS * LANES
    nh = 2 * SB_PAIRS
    q3, k3, v3 = (t.reshape(batch, s, d) for t in (q, k, v))
    from_key = jnp.tril(jnp.ones((SB_T, SB_T), BF16))
    out = pl.pallas_call(
        _sb_kernel,
        grid=(batch, d // width, s // SB_T),
        in_specs=[pl.BlockSpec((None, SB_T, width), lambda b, p, i: (b, i, p)),
                  pl.BlockSpec((None, s, width), lambda b, p, i: (b, 0, p)),
                  pl.BlockSpec((None, s, width), lambda b, p, i: (b, 0, p)),
                  pl.BlockSpec((SB_T, SB_T), lambda b, p, i: (0, 0))],
        out_specs=pl.BlockSpec((None, SB_T, width), lambda b, p, i: (b, i, p)),
        out_shape=jax.ShapeDtypeStruct((batch, s, d), BF16),
        scratch_shapes=[pltpu.VMEM((nh, SB_T, LANES), BF16),
                        pltpu.VMEM((2, nh, SB_T, SB_T), F32),
                        pltpu.VMEM((2, nh, SB_T, SB_T), BF16),
                        pltpu.VMEM((nh, SB_T, LANES), F32),
                        pltpu.VMEM((nh, SB_T, 1), F32)],
        compiler_params=_params(("parallel", "parallel", "arbitrary")),
        name=name,
    )(q3, k3, v3, from_key)
    return out.reshape(n, d)


def kernel(x, e_norm1, e_w_in, e_sgu_ln_g, e_sgu_ln_b, e_sgu_w, e_sgu_b, e_conv_w, e_a_log, e_dt_bias, e_o_norm, e_w_out, e_norm2, e_ffn_w1, e_ffn_w3, e_ffn_w2, o_norm1, o_w_qkv, o_w_out, o_norm2, o_router, o_moe_w1, o_moe_w3, o_moe_w2, final_norm):
    batch, seq, d = x.shape
    n = batch * seq
    h = x.reshape(n, d)
    n_main = 2 * SGU_WIDTH + CONV_CH + GDN_V
    depth = e_norm1.shape[0] + o_norm1.shape[0]
    for layer in range(depth):
        i = layer // 2
        if layer % 2 == 0:
            w_in = e_w_in[i]
            w_in = jnp.concatenate(
                [w_in, jnp.zeros((d, LANES - (w_in.shape[1] - n_main)), F32)], axis=1).astype(BF16)
            splits = (0, 2 * SGU_WIDTH, 2 * SGU_WIDTH + CONV_CH, n_main, n_main + LANES)
            uv, qkv, gate, ba = norm_proj(h, e_norm1[i], w_in, splits, (F32, F32, F32, F32), 512,
                                          f"even{i}_in_proj")
            y_a = sgu(uv, e_sgu_ln_g[i], e_sgu_ln_b[i], e_sgu_w[i], e_sgu_b[i], f"even{i}_sgu")
            y_b = gdn(qkv, gate, ba, e_conv_w[i], e_a_log[i], e_dt_bias[i], e_o_norm[i], batch,
                      f"even{i}_gdn")
            h = proj_residual(h, e_w_out[i].astype(BF16), [y_a, y_b], 1024, f"even{i}_out_proj")
            h = ffn_residual(h, e_norm2[i], e_ffn_w1[i].astype(BF16), e_ffn_w3[i].astype(BF16),
                             e_ffn_w2[i].astype(BF16), 1024, 256, f"even{i}_ffn")
        else:
            q, k, v = norm_proj(h, o_norm1[i], o_w_qkv[i].astype(BF16), (0, d, 2 * d, 3 * d),
                                (BF16, BF16, BF16), 512, f"odd{i}_qkv_proj")
            att = stick_breaking(q, k, v, batch, f"odd{i}_stick_breaking")
            h = proj_residual(h, o_w_out[i].astype(BF16), [att], 1024, f"odd{i}_out_proj")
            last = layer == depth - 1
            h = moe_residual(h, o_norm2[i], o_router[i], o_moe_w1[i].astype(BF16), o_moe_w3[i].astype(BF16),
                             o_moe_w2[i].astype(BF16), final_norm if last else None, f"odd{i}_moe")
    if depth % 2 == 1:
        h = final_norm_call(h, final_norm, 1024, "final_norm")
    return h.reshape(batch, seq, d)
```

```python
import functools

import jax
import jax.numpy as jnp
from jax import lax
from jax.experimental import pallas as pl
from jax.experimental.pallas import tpu as pltpu

F32 = jnp.float32
BF16 = jnp.bfloat16

D_MODEL = 1024
EPS = 1e-6
CHUNK = 64
SGU_BLOCK = 128
SGU_GROUPS = 8
SGU_WIDTH = 512
SGU_GDIM = 64
GDN_HEADS = 4
GDN_DK = 128
GDN_DV = 128
GDN_CONV = 4
GDN_QK = 512
GDN_V = 512
CONV_CH = 1536
SB_HEADS = 16
SB_DIM = 64
D_FF = 2816
N_EXPERTS = 8
LANES = 128
VMEM_LIMIT = 48 * 1024 * 1024


def _params(sem):
    return pltpu.CompilerParams(dimension_semantics=sem, vmem_limit_bytes=VMEM_LIMIT)


def _rms(x, g):
    return x * lax.rsqrt(jnp.mean(x * x, axis=-1, keepdims=True) + EPS) * g


def _sigmoid(x):
    return 1.0 / (1.0 + jnp.exp(-x))


def _softplus(x):
    return jnp.maximum(x, 0.0) + jnp.log1p(jnp.exp(-jnp.abs(x)))


def _gelu(x):
    return 0.5 * x * (1.0 + jnp.tanh(0.7978845608028654 * (x + 0.044715 * (x * x * x))))


def _dot(a, b):
    return jnp.dot(a, b, preferred_element_type=F32)


def _dot_nt(a, b):
    return lax.dot_general(a, b, (((1,), (1,)), ((), ())), preferred_element_type=F32)


def _dot_tn(a, b):
    return lax.dot_general(a, b, (((0,), (0,)), ((), ())), preferred_element_type=F32)


def _norm_proj_kernel(h_ref, g_ref, w_ref, *o_refs, splits):
    xn = _rms(h_ref[...], g_ref[...]).astype(BF16)
    for o_ref, a, b in zip(o_refs, splits[:-1], splits[1:]):
        o_ref[...] = _dot(xn, w_ref[:, a:b]).astype(o_ref.dtype)


def norm_proj(h, g, w, splits, dtypes, tm, name):
    n, d = h.shape
    outs = [jax.ShapeDtypeStruct((n, b - a), dt) for a, b, dt in zip(splits[:-1], splits[1:], dtypes)]
    return pl.pallas_call(
        functools.partial(_norm_proj_kernel, splits=splits),
        grid=(n // tm,),
        in_specs=[pl.BlockSpec((tm, d), lambda i: (i, 0)),
                  pl.BlockSpec((1, d), lambda i: (0, 0)),
                  pl.BlockSpec(w.shape, lambda i: (0, 0))],
        out_specs=[pl.BlockSpec((tm, o.shape[1]), lambda i: (i, 0)) for o in outs],
        out_shape=outs,
        compiler_params=_params(("parallel",)),
        name=name,
    )(h, g.reshape(1, d), w)


def _proj_res_kernel(h_ref, w_ref, *refs, splits):
    x_refs, o_ref = refs[:-1], refs[-1]
    acc = h_ref[...]
    for x_ref, a, b in zip(x_refs, splits[:-1], splits[1:]):
        acc = acc + _dot(x_ref[...], w_ref[a:b, :])
    o_ref[...] = acc


def proj_residual(h, w, xs, tm, name):
    n, d = h.shape
    splits = [0]
    for x in xs:
        splits.append(splits[-1] + x.shape[1])
    return pl.pallas_call(
        functools.partial(_proj_res_kernel, splits=tuple(splits)),
        grid=(n // tm,),
        in_specs=[pl.BlockSpec((tm, d), lambda i: (i, 0)),
                  pl.BlockSpec(w.shape, lambda i: (0, 0))]
                 + [pl.BlockSpec((tm, x.shape[1]), lambda i: (i, 0)) for x in xs],
        out_specs=pl.BlockSpec((tm, d), lambda i: (i, 0)),
        out_shape=jax.ShapeDtypeStruct((n, d), F32),
        compiler_params=_params(("parallel",)),
        name=name,
    )(h, w, *xs)


def _ffn_kernel(h_ref, g_ref, w1_ref, w3_ref, w2_ref, o_ref, xn_ref, acc_ref):
    j = pl.program_id(1)

    @pl.when(j == 0)
    def _():
        xn_ref[...] = _rms(h_ref[...], g_ref[...]).astype(BF16)
        acc_ref[...] = jnp.zeros_like(acc_ref)

    xn = xn_ref[...]
    a = _dot(xn, w1_ref[...])
    b = _dot(xn, w3_ref[...])
    hidden = (a * _sigmoid(a) * b).astype(BF16)
    acc_ref[...] += _dot(hidden, w2_ref[...])

    @pl.when(j == pl.num_programs(1) - 1)
    def _():
        o_ref[...] = h_ref[...] + acc_ref[...]


def ffn_residual(h, g, w1, w3, w2, tm, tf, name):
    n, d = h.shape
    f = w1.shape[1]
    return pl.pallas_call(
        _ffn_kernel,
        grid=(n // tm, f // tf),
        in_specs=[pl.BlockSpec((tm, d), lambda i, j: (i, 0)),
                  pl.BlockSpec((1, d), lambda i, j: (0, 0)),
                  pl.BlockSpec((d, tf), lambda i, j: (0, j)),
                  pl.BlockSpec((d, tf), lambda i, j: (0, j)),
                  pl.BlockSpec((tf, d), lambda i, j: (j, 0))],
        out_specs=pl.BlockSpec((tm, d), lambda i, j: (i, 0)),
        out_shape=jax.ShapeDtypeStruct((n, d), F32),
        scratch_shapes=[pltpu.VMEM((tm, d), BF16), pltpu.VMEM((tm, d), F32)],
        compiler_params=_params(("parallel", "arbitrary")),
        name=name,
    )(h, g.reshape(1, d), w1, w3, w2)


MOE_TM = 512
ROUTER_TM = 256
ROW_WAIT_UNROLL = 16
META_E1, META_E2, META_R1, META_R2, META_G1, META_G2 = range(6)


def _router_kernel(h_ref, g_ref, wr_ref, meta_ref, cnt_ref, run_ref):
    @pl.when(pl.program_id(0) == 0)
    def _():
        run_ref[...] = jnp.zeros_like(run_ref)

    xn = _rms(h_ref[...], g_ref[...])
    logits = jnp.dot(xn, wr_ref[...], preferred_element_type=F32,
                     precision=lax.Precision.HIGHEST)
    tm = logits.shape[0]
    lane = lax.broadcasted_iota(jnp.int32, logits.shape, 1)
    neg = jnp.float32(-jnp.inf)
    l1 = jnp.where(lane < N_EXPERTS, logits, neg)
    m1 = jnp.max(l1, axis=-1, keepdims=True)
    i1 = jnp.min(jnp.where(l1 == m1, lane, LANES), axis=-1, keepdims=True)
    l2 = jnp.where(lane == i1, neg, l1)
    m2 = jnp.max(l2, axis=-1, keepdims=True)
    i2 = jnp.min(jnp.where(l2 == m2, lane, LANES), axis=-1, keepdims=True)
    e2 = jnp.exp(m2 - m1)
    g1 = 1.0 / (1.0 + e2)
    g2 = e2 / (1.0 + e2)

    chosen = (lane == i1) | (lane == i2)
    onehot = jnp.where(chosen, 1.0, 0.0)
    rr = lax.broadcasted_iota(jnp.int32, (tm, tm), 0)
    cc = lax.broadcasted_iota(jnp.int32, (tm, tm), 1)
    before = jnp.where(cc < rr, 1.0, 0.0).astype(BF16)
    prefix = _dot(before, onehot.astype(BF16)) + run_ref[0:1, :]
    r1 = jnp.sum(jnp.where(lane == i1, prefix, 0.0), axis=-1, keepdims=True)
    r2 = jnp.sum(jnp.where(lane == i2, prefix, 0.0), axis=-1, keepdims=True)
    run_ref[...] = run_ref[...] + jnp.sum(onehot, axis=0, keepdims=True)
    cnt_ref[...] = run_ref[...]

    meta = jnp.zeros(logits.shape, F32)
    for idx, val in ((META_E1, i1.astype(F32)), (META_E2, i2.astype(F32)), (META_R1, r1), (META_R2, r2),
                     (META_G1, g1), (META_G2, g2)):
        meta = jnp.where(lane == idx, val, meta)
    meta_ref[...] = meta


def router(h, g, wr_pad, name):
    n, d = h.shape
    tm = ROUTER_TM
    return pl.pallas_call(
        _router_kernel,
        grid=(n // tm,),
        in_specs=[pl.BlockSpec((tm, d), lambda i: (i, 0)),
                  pl.BlockSpec((1, d), lambda i: (0, 0)),
                  pl.BlockSpec((d, LANES), lambda i: (0, 0))],
        out_specs=[pl.BlockSpec((tm, LANES), lambda i: (i, 0)),
                   pl.BlockSpec((8, LANES), lambda i: (0, 0))],
        out_shape=[jax.ShapeDtypeStruct((n, LANES), F32), jax.ShapeDtypeStruct((8, LANES), F32)],
        scratch_shapes=[pltpu.VMEM((8, LANES), F32)],
        compiler_params=_params(("arbitrary",)),
        name=name,
    )(h, g.reshape(1, d), wr_pad)


def _dispatch_kernel(pos1_ref, pos2_ref, h_ref, xs_init_hbm, xs_hbm, sem, *, tm):
    del xs_init_hbm

    def copies(r):
        src = h_ref.at[pl.ds(r, 1)]
        return (pltpu.make_async_copy(src, xs_hbm.at[pl.ds(pos1_ref[r], 1)], sem.at[0]),
                pltpu.make_async_copy(src, xs_hbm.at[pl.ds(pos2_ref[r], 1)], sem.at[1]))

    def start(r, carry):
        for cp in copies(r):
            cp.start()
        return carry

    def wait(r, carry):
        for cp in copies(r):
            cp.wait()
        return carry

    lax.fori_loop(0, tm, start, 0)
    lax.fori_loop(0, tm, wait, 0, unroll=ROW_WAIT_UNROLL)


def dispatch(h, pos1, pos2, n_rows, tm, name):
    n, d = h.shape
    return pl.pallas_call(
        functools.partial(_dispatch_kernel, tm=tm),
        grid=(n // tm,),
        in_specs=[pl.BlockSpec((tm,), lambda i: (i,), memory_space=pltpu.SMEM),
                  pl.BlockSpec((tm,), lambda i: (i,), memory_space=pltpu.SMEM),
                  pl.BlockSpec((tm, d), lambda i: (i, 0)),
                  pl.BlockSpec(memory_space=pl.ANY)],
        out_specs=pl.BlockSpec(memory_space=pl.ANY),
        out_shape=jax.ShapeDtypeStruct((n_rows, d), F32),
        scratch_shapes=[pltpu.SemaphoreType.DMA((2,))],
        input_output_aliases={3: 0},
        compiler_params=_params(("arbitrary",)),
        name=name,
    )(pos1, pos2, h, jnp.zeros((n_rows, d), F32))


def _experts_kernel(te_ref, nt_ref, xs_ref, g_ref, w1_ref, w3_ref, w2_ref, ys_ref, xn_ref, acc_ref):
    del te_ref
    i = pl.program_id(0)
    j = pl.program_id(1)

    @pl.when(i < nt_ref[0])
    def _():
        @pl.when(j == 0)
        def _():
            xn_ref[...] = _rms(xs_ref[...], g_ref[...]).astype(BF16)
            acc_ref[...] = jnp.zeros_like(acc_ref)

        xn = xn_ref[...]
        a = _dot(xn, w1_ref[...])
        b = _dot(xn, w3_ref[...])
        hidden = (a * _sigmoid(a) * b).astype(BF16)
        acc_ref[...] += _dot(hidden, w2_ref[...])

        @pl.when(j == pl.num_programs(1) - 1)
        def _():
            ys_ref[...] = acc_ref[...]

    @pl.when((i >= nt_ref[0]) & (j == pl.num_programs(1) - 1))
    def _():
        ys_ref[...] = jnp.zeros_like(ys_ref)


def experts(xs, g, tile_expert, n_tiles, w1, w3, w2, layer, tf, name):
    n_rows, d = xs.shape
    f = w1.shape[3]
    tm = MOE_TM
    nj = f // tf

    def used(i, nt):
        return jnp.maximum(jnp.minimum(i, nt[0] - 1), 0)

    def row_map(i, j, te, nt):
        return (used(i, nt), 0)

    def up_map(i, j, te, nt):
        return (layer, te[used(i, nt)], 0, jnp.where(i < nt[0], j, nj - 1))

    def down_map(i, j, te, nt):
        return (layer, te[used(i, nt)], jnp.where(i < nt[0], j, nj - 1), 0)

    return pl.pallas_call(
        _experts_kernel,
        grid_spec=pltpu.PrefetchScalarGridSpec(
            num_scalar_prefetch=2,
            grid=(n_rows // tm, nj),
            in_specs=[pl.BlockSpec((tm, d), row_map),
                      pl.BlockSpec((1, d), lambda i, j, te, nt: (0, 0)),
                      pl.BlockSpec((None, None, d, tf), up_map),
                      pl.BlockSpec((None, None, d, tf), up_map),
                      pl.BlockSpec((None, None, tf, d), down_map)],
            out_specs=pl.BlockSpec((tm, d), lambda i, j, te, nt: (i, 0)),
            scratch_shapes=[pltpu.VMEM((tm, d), BF16), pltpu.VMEM((tm, d), F32)]),
        out_shape=jax.ShapeDtypeStruct((n_rows, d), F32),
        compiler_params=_params(("arbitrary", "arbitrary")),
        name=name,
    )(tile_expert, n_tiles, xs, g.reshape(1, d), w1, w3, w2)


def _combine_kernel(pos1_ref, pos2_ref, meta_ref, h_ref, gf_ref, ys_hbm, o_ref, buf_ref, sem, *, tm, final):
    def copies(r):
        return (pltpu.make_async_copy(ys_hbm.at[pl.ds(pos1_ref[r], 1)], buf_ref.at[0, pl.ds(r, 1)], sem.at[0]),
                pltpu.make_async_copy(ys_hbm.at[pl.ds(pos2_ref[r], 1)], buf_ref.at[1, pl.ds(r, 1)], sem.at[1]))

    def start(r, carry):
        for cp in copies(r):
            cp.start()
        return carry

    def wait(r, carry):
        for cp in copies(r):
            cp.wait()
        return carry

    lax.fori_loop(0, tm, start, 0)
    lax.fori_loop(0, tm, wait, 0, unroll=ROW_WAIT_UNROLL)
    meta = meta_ref[...]
    out = (h_ref[...] + meta[:, META_G1:META_G1 + 1] * buf_ref[0]
           + meta[:, META_G2:META_G2 + 1] * buf_ref[1])
    if final:
        out = _rms(out, gf_ref[...])
    o_ref[...] = out


def combine(h, meta, pos1, pos2, ys, final_g, tm, name):
    n, d = h.shape
    final = final_g is not None
    gf = (final_g if final else jnp.ones((d,), F32)).reshape(1, d)
    return pl.pallas_call(
        functools.partial(_combine_kernel, tm=tm, final=final),
        grid=(n // tm,),
        in_specs=[pl.BlockSpec((tm,), lambda i: (i,), memory_space=pltpu.SMEM),
                  pl.BlockSpec((tm,), lambda i: (i,), memory_space=pltpu.SMEM),
                  pl.BlockSpec((tm, LANES), lambda i: (i, 0)),
                  pl.BlockSpec((tm, d), lambda i: (i, 0)),
                  pl.BlockSpec((1, d), lambda i: (0, 0)),
                  pl.BlockSpec(memory_space=pl.ANY)],
        out_specs=pl.BlockSpec((tm, d), lambda i: (i, 0)),
        out_shape=jax.ShapeDtypeStruct((n, d), F32),
        scratch_shapes=[pltpu.VMEM((2, tm, d), F32), pltpu.SemaphoreType.DMA((2,))],
        compiler_params=_params(("arbitrary",)),
        name=name,
    )(pos1, pos2, meta, h, gf, ys)


def moe_residual(h, g, w_router, w1, w3, w2, layer, final_g, name):
    n, d = h.shape
    wr = jnp.concatenate([w_router, jnp.zeros((d, LANES - N_EXPERTS), F32)], axis=1)
    meta, cnt = router(h, g, wr, f"{name}_router")
    cnt = cnt[0, :N_EXPERTS].astype(jnp.int32)
    n_tile_e = (cnt + MOE_TM - 1) // MOE_TM
    tile_end = jnp.cumsum(n_tile_e)
    offset = (tile_end - n_tile_e) * MOE_TM
    e1 = meta[:, META_E1].astype(jnp.int32)
    e2 = meta[:, META_E2].astype(jnp.int32)
    pos1 = offset[e1] + meta[:, META_R1].astype(jnp.int32)
    pos2 = offset[e2] + meta[:, META_R2].astype(jnp.int32)
    max_tiles = (2 * n) // MOE_TM + N_EXPERTS
    tile_ids = jnp.arange(max_tiles, dtype=jnp.int32)
    tile_expert = jnp.minimum(jnp.sum(tile_ids[:, None] >= tile_end[None, :], axis=1), N_EXPERTS - 1)
    n_tiles = tile_end[-1:].astype(jnp.int32)
    xs = dispatch(h, pos1, pos2, max_tiles * MOE_TM, 1024, f"{name}_dispatch")
    ys = experts(xs, g, tile_expert.astype(jnp.int32), n_tiles, w1, w3, w2, layer, 1408, f"{name}_experts")
    return combine(h, meta, pos1, pos2, ys, final_g, 512, f"{name}_combine")


def _final_norm_kernel(h_ref, g_ref, o_ref):
    o_ref[...] = _rms(h_ref[...], g_ref[...])


def final_norm_call(h, g, tm, name):
    n, d = h.shape
    return pl.pallas_call(
        _final_norm_kernel,
        grid=(n // tm,),
        in_specs=[pl.BlockSpec((tm, d), lambda i: (i, 0)),
                  pl.BlockSpec((1, d), lambda i: (0, 0))],
        out_specs=pl.BlockSpec((tm, d), lambda i: (i, 0)),
        out_shape=jax.ShapeDtypeStruct((n, d), F32),
        compiler_params=_params(("parallel",)),
        name=name,
    )(h, g.reshape(1, d))


def _sgu_kernel(uv_ref, lng_ref, lnb_ref, wcat_ref, bias_ref, o_ref):
    uv = uv_ref[...]
    u = _gelu(uv[:, :SGU_WIDTH])
    v = _gelu(uv[:, SGU_WIDTH:])
    mu = jnp.mean(v, axis=-1, keepdims=True)
    vc = v - mu
    var = jnp.mean(vc * vc, axis=-1, keepdims=True)
    v = (vc * lax.rsqrt(var + EPS) * lng_ref[...] + lnb_ref[...]).astype(BF16)

    kdim = SGU_GROUPS * SGU_BLOCK
    row_g = lax.broadcasted_iota(jnp.int32, (kdim, SGU_WIDTH), 0) // SGU_BLOCK
    col_g = lax.broadcasted_iota(jnp.int32, (kdim, SGU_WIDTH), 1) // SGU_GDIM
    vbd = jnp.where(row_g == col_g, jnp.concatenate([v] * SGU_GROUPS, axis=0), jnp.zeros((), BF16))

    t_chunk = lax.broadcasted_iota(jnp.int32, (SGU_BLOCK, kdim), 0) // CHUNK
    s_chunk = (lax.broadcasted_iota(jnp.int32, (SGU_BLOCK, kdim), 1) % SGU_BLOCK) // CHUNK
    w = jnp.where(s_chunk <= t_chunk, wcat_ref[...], 0.0).astype(BF16)
    mixed = _dot(w, vbd) + bias_ref[...]
    o_ref[...] = (u * mixed).astype(o_ref.dtype)


def sgu(uv, ln_g, ln_b, w_s, b_s, name):
    n = uv.shape[0]
    wcat = jnp.transpose(w_s, (1, 0, 2)).reshape(SGU_BLOCK, SGU_GROUPS * SGU_BLOCK)
    bias = jnp.repeat(b_s.T, SGU_GDIM, axis=1)
    return pl.pallas_call(
        _sgu_kernel,
        grid=(n // SGU_BLOCK,),
        in_specs=[pl.BlockSpec((SGU_BLOCK, 2 * SGU_WIDTH), lambda i: (i, 0)),
                  pl.BlockSpec((1, SGU_WIDTH), lambda i: (0, 0)),
                  pl.BlockSpec((1, SGU_WIDTH), lambda i: (0, 0)),
                  pl.BlockSpec(wcat.shape, lambda i: (0, 0)),
                  pl.BlockSpec(bias.shape, lambda i: (0, 0))],
        out_specs=pl.BlockSpec((SGU_BLOCK, SGU_WIDTH), lambda i: (i, 0)),
        out_shape=jax.ShapeDtypeStruct((n, SGU_WIDTH), BF16),
        compiler_params=_params(("parallel",)),
        name=name,
    )(uv, ln_g.reshape(1, -1), ln_b.reshape(1, -1), wcat, bias)


GDN_T = 256
HALO = 8


def _bdot(a, b):
    return lax.dot_general(a, b, (((2,), (1,)), ((0,), (0,))), preferred_element_type=F32)


def _bdot_nt(a, b):
    return lax.dot_general(a, b, (((2,), (2,)), ((0,), (0,))), preferred_element_type=F32)


def _bdot_tn(a, b):
    return lax.dot_general(a, b, (((1,), (1,)), ((0,), (0,))), preferred_element_type=F32)


def _split3(x):
    hi = x.astype(BF16)
    r1 = x - hi.astype(F32)
    mid = r1.astype(BF16)
    lo = (r1 - mid.astype(F32)).astype(BF16)
    return hi, mid, lo


def _gdn_kernel(qkv_ref, gate_ref, ba_ref, convw_ref, arow_ref, onorm_ref, o_ref, state_ref, xbuf_ref):
    tb = pl.program_id(1)
    nc = GDN_T // CHUNK
    nu = nc * GDN_HEADS

    @pl.when(tb == 0)
    def _():
        state_ref[...] = jnp.zeros_like(state_ref)
        xbuf_ref[0:HALO, :] = jnp.zeros((HALO, CONV_CH), F32)

    @pl.when(tb != 0)
    def _():
        xbuf_ref[0:HALO, :] = xbuf_ref[GDN_T:GDN_T + HALO, :]

    xbuf_ref[HALO:HALO + GDN_T, :] = qkv_ref[...]
    cw = convw_ref[...]
    y = cw[0:1, :] * xbuf_ref[pl.ds(HALO - 3, GDN_T), :]
    for i in range(1, GDN_CONV):
        y = y + cw[i:i + 1, :] * xbuf_ref[pl.ds(HALO - 3 + i, GDN_T), :]
    qkv = y * _sigmoid(y)

    arow = arow_ref[...]
    ba = ba_ref[...]
    beta_full = _sigmoid(ba)
    g_full = arow[0:1, :] * _softplus(ba + arow[1:2, :])

    ii = lax.broadcasted_iota(jnp.int32, (CHUNK, CHUNK), 0)
    jj = lax.broadcasted_iota(jnp.int32, (CHUNK, CHUNK), 1)
    causal = ii >= jj
    strict = ii > jj

    tril = jnp.broadcast_to(jnp.where(causal, 1.0, 0.0).astype(BF16), (nc, CHUNK, CHUNK))
    gc_full = sum(_bdot(tril, part) for part in _split3(g_full.reshape(nc, CHUNK, LANES)))

    def units(fn):
        return jnp.stack([fn(c, h) for c in range(nc) for h in range(GDN_HEADS)], axis=0)

    def rows(c):
        return slice(c * CHUNK, (c + 1) * CHUNK)

    def l2n(x):
        return x * lax.rsqrt(jnp.sum(x * x, axis=-1, keepdims=True) + EPS)

    q = l2n(units(lambda c, h: qkv[rows(c), h * GDN_DK:(h + 1) * GDN_DK])) * (GDN_DK ** -0.5)
    k = l2n(units(lambda c, h: qkv[rows(c), GDN_QK + h * GDN_DK:GDN_QK + (h + 1) * GDN_DK]))
    v = units(lambda c, h: qkv[rows(c), 2 * GDN_QK + h * GDN_DV:2 * GDN_QK + (h + 1) * GDN_DV])
    beta = units(lambda c, h: beta_full[rows(c), h:h + 1])
    gc = units(lambda c, h: gc_full[c, :, 4 + h:5 + h])

    pick0 = jnp.broadcast_to(jnp.where(jj == 0, 1.0, 0.0).astype(BF16), (nu, CHUNK, CHUNK))
    gc_b = jnp.broadcast_to(gc, (nu, CHUNK, CHUNK))
    gc_row = sum(_bdot_nt(pick0, part) for part in _split3(gc_b))
    gamma = jnp.where(causal, jnp.exp(jnp.where(causal, gc - gc_row, 0.0)), 0.0)
    g_last = gc[:, CHUNK - 1:CHUNK, :]
    egc = jnp.exp(gc)

    kb = k * beta
    kbf = k.astype(BF16)
    lmat = jnp.where(strict, _bdot_nt(kb.astype(BF16), kbf) * gamma, 0.0)
    attn = (_bdot_nt(q.astype(BF16), kbf) * gamma).astype(BF16)
    x = jnp.concatenate([v * beta, kb * egc], axis=2)
    npow = -lmat
    for level in range(6):
        nb = npow.astype(BF16)
        x = x + _bdot(nb, x.astype(BF16))
        if level < 5:
            npow = _bdot(nb, nb)
    u = x[:, :, :GDN_DV].astype(BF16)
    w = x[:, :, GDN_DV:].astype(BF16)
    k_dec = (k * jnp.exp(g_last - gc)).astype(BF16)
    q_eff = (q * egc - _bdot(attn, w)).astype(BF16)
    o_loc = _bdot(attn, u)
    s_mix = _bdot_tn(k_dec, w).astype(BF16)
    s_add = _bdot_tn(k_dec, u)
    decay = jnp.exp(g_last)

    onorm = onorm_ref[...]
    gate = gate_ref[...]
    state = state_ref[...]
    for c in range(nc):
        sl = slice(c * GDN_HEADS, (c + 1) * GDN_HEADS)
        sb = state.astype(BF16)
        o = _bdot(q_eff[sl], sb) + o_loc[sl]
        state = decay[sl] * state - _bdot(s_mix[sl], sb) + s_add[sl]
        o = o * lax.rsqrt(jnp.mean(o * o, axis=-1, keepdims=True) + EPS) * onorm
        for h in range(GDN_HEADS):
            gt = gate[rows(c), h * GDN_DV:(h + 1) * GDN_DV]
            o_ref[rows(c), h * GDN_DV:(h + 1) * GDN_DV] = (o[h] * (gt * _sigmoid(gt))).astype(o_ref.dtype)
    state_ref[...] = state


def gdn(qkv, gate, ba, conv_w, a_log, dt_bias, o_norm, batch, name):
    n = qkv.shape[0]
    s = n // batch
    nt = s // GDN_T
    arow = jnp.zeros((8, LANES), F32).at[0, 4:8].set(-jnp.exp(a_log)).at[1, 4:8].set(dt_bias)
    return pl.pallas_call(
        _gdn_kernel,
        grid=(batch, nt),
        in_specs=[pl.BlockSpec((GDN_T, CONV_CH), lambda b, t: (b * nt + t, 0)),
                  pl.BlockSpec((GDN_T, GDN_V), lambda b, t: (b * nt + t, 0)),
                  pl.BlockSpec((GDN_T, LANES), lambda b, t: (b * nt + t, 0)),
                  pl.BlockSpec((GDN_CONV, CONV_CH), lambda b, t: (0, 0)),
                  pl.BlockSpec((8, LANES), lambda b, t: (0, 0)),
                  pl.BlockSpec((1, GDN_DV), lambda b, t: (0, 0))],
        out_specs=pl.BlockSpec((GDN_T, GDN_V), lambda b, t: (b * nt + t, 0)),
        out_shape=jax.ShapeDtypeStruct((n, GDN_V), BF16),
        scratch_shapes=[pltpu.VMEM((GDN_HEADS, GDN_DK, GDN_DV), F32),
                        pltpu.VMEM((HALO + GDN_T, CONV_CH), F32)],
        compiler_params=_params(("parallel", "arbitrary")),
        name=name,
    )(qkv, gate, ba, conv_w, arow, o_norm.reshape(1, -1))


SB_T = 256
SB_PAIRS = 4
SB_DEAD = 120.0


def _sb_kernel(q_ref, k_ref, v_ref, fk_ref, o_ref, qh_ref, z_ref, sp_ref, acc_ref, cs_ref):
    qi = pl.program_id(2)
    lane = lax.broadcasted_iota(jnp.int32, (1, LANES), 1)
    heads = [(p, hh) for p in range(SB_PAIRS) for hh in range(2)]

    for idx, (p, hh) in enumerate(heads):
        q = q_ref[:, p * LANES:(p + 1) * LANES]
        in_head = (lane >= hh * SB_DIM) & (lane < (hh + 1) * SB_DIM)
        qh_ref[idx] = jnp.where(in_head, q, jnp.zeros((), BF16)) * jnp.asarray(SB_DIM ** -0.5, BF16)
    acc_ref[...] = jnp.zeros_like(acc_ref)
    cs_ref[...] = jnp.zeros_like(cs_ref)

    def earlier():
        row = lax.broadcasted_iota(jnp.int32, (SB_T, SB_T), 0)
        col = lax.broadcasted_iota(jnp.int32, (SB_T, SB_T), 1)
        return col < row

    def stage_a(kb, slot, diagonal=False):
        start = pl.multiple_of(kb * SB_T, SB_T)
        for idx, (p, hh) in enumerate(heads):
            kblk = k_ref[pl.ds(start, SB_T), p * LANES:(p + 1) * LANES]
            z = _dot_nt(qh_ref[idx], kblk)
            sp = jnp.maximum(z, 0.0) + jnp.log(1.0 + jnp.exp(-jnp.abs(z)))
            if diagonal:
                sp = jnp.where(earlier(), sp, 0.0)
            z_ref[slot, idx] = z
            sp_ref[slot, idx] = sp.astype(BF16)

    def stage_b(kb, slot, diagonal=False):
        start = pl.multiple_of(kb * SB_T, SB_T)
        for idx, (p, hh) in enumerate(heads):
            vblk = v_ref[pl.ds(start, SB_T), p * LANES:(p + 1) * LANES]
            ssum = _dot(sp_ref[slot, idx], fk_ref[...])
            att = jnp.exp(jnp.minimum(z_ref[slot, idx] - ssum, 0.0) - cs_ref[idx])
            if diagonal:
                att = jnp.where(earlier(), att, 0.0)
            acc_ref[idx] += _dot(att.astype(BF16), vblk)
            cs_ref[idx] += ssum[:, 0:1]

    def live():
        return (jnp.min(cs_ref[...]) < SB_DEAD).astype(jnp.int32)

    @pl.when(qi == 0)
    def _():
        stage_a(0, 0, True)
        stage_b(0, 0, True)

    @pl.when(qi > 0)
    def _():
        stage_a(qi, 0, True)
        stage_a(qi - 1, 1)
        stage_b(qi, 0, True)
        stage_b(qi - 1, 1)

        def more(carry):
            kb, go = carry
            return (kb >= 0) & (go > 0)

        def one_tile(carry):
            kb, _ = carry
            stage_a(kb, 0)
            stage_b(kb, 0)
            return kb - 1, live()

        lax.while_loop(more, one_tile, (qi - 2, live()))

    for p in range(SB_PAIRS):
        o_ref[:, p * LANES:(p + 1) * LANES] = jnp.where(
            lane < SB_DIM, acc_ref[2 * p], acc_ref[2 * p + 1]).astype(o_ref.dtype)


def stick_breaking(q, k, v, batch, name):
    n, d = q.shape
    s = n // batch
    width = SB_PAIRS * LANES
    nh = 2 * SB_PAIRS
    q3, k3, v3 = (t.reshape(batch, s, d) for t in (q, k, v))
    from_key = jnp.tril(jnp.ones((SB_T, SB_T), BF16))
    out = pl.pallas_call(
        _sb_kernel,
        grid=(batch, d // width, s // SB_T),
        in_specs=[pl.BlockSpec((None, SB_T, width), lambda b, p, i: (b, i, p)),
                  pl.BlockSpec((None, s, width), lambda b, p, i: (b, 0, p)),
                  pl.BlockSpec((None, s, width), lambda b, p, i: (b, 0, p)),
                  pl.BlockSpec((SB_T, SB_T), lambda b, p, i: (0, 0))],
        out_specs=pl.BlockSpec((None, SB_T, width), lambda b, p, i: (b, i, p)),
        out_shape=jax.ShapeDtypeStruct((batch, s, d), BF16),
        scratch_shapes=[pltpu.VMEM((nh, SB_T, LANES), BF16),
                        pltpu.VMEM((2, nh, SB_T, SB_T), F32),
                        pltpu.VMEM((2, nh, SB_T, SB_T), BF16),
                        pltpu.VMEM((nh, SB_T, LANES), F32),
                        pltpu.VMEM((nh, SB_T, 1), F32)],
        compiler_params=_params(("parallel", "parallel", "arbitrary")),
        name=name,
    )(q3, k3, v3, from_key)
    return out.reshape(n, d)


def kernel(x, e_norm1, e_w_in, e_sgu_ln_g, e_sgu_ln_b, e_sgu_w, e_sgu_b, e_conv_w, e_a_log, e_dt_bias, e_o_norm, e_w_out, e_norm2, e_ffn_w1, e_ffn_w3, e_ffn_w2, o_norm1, o_w_qkv, o_w_out, o_norm2, o_router, o_moe_w1, o_moe_w3, o_moe_w2, final_norm):
    batch, seq, d = x.shape
    n = batch * seq
    h = x.reshape(n, d)
    n_main = 2 * SGU_WIDTH + CONV_CH + GDN_V
    depth = e_norm1.shape[0] + o_norm1.shape[0]
    moe_w1, moe_w3, moe_w2 = (w.astype(BF16) for w in (o_moe_w1, o_moe_w3, o_moe_w2))
    for layer in range(depth):
        i = layer // 2
        if layer % 2 == 0:
            w_in = e_w_in[i]
            w_in = jnp.concatenate(
                [w_in, jnp.zeros((d, LANES - (w_in.shape[1] - n_main)), F32)], axis=1).astype(BF16)
            splits = (0, 2 * SGU_WIDTH, 2 * SGU_WIDTH + CONV_CH, n_main, n_main + LANES)
            uv, qkv, gate, ba = norm_proj(h, e_norm1[i], w_in, splits, (F32, F32, F32, F32), 512,
                                          f"even{i}_in_proj")
            y_a = sgu(uv, e_sgu_ln_g[i], e_sgu_ln_b[i], e_sgu_w[i], e_sgu_b[i], f"even{i}_sgu")
            y_b = gdn(qkv, gate, ba, e_conv_w[i], e_a_log[i], e_dt_bias[i], e_o_norm[i], batch,
                      f"even{i}_gdn")
            h = proj_residual(h, e_w_out[i].astype(BF16), [y_a, y_b], 1024, f"even{i}_out_proj")
            h = ffn_residual(h, e_norm2[i], e_ffn_w1[i].astype(BF16), e_ffn_w3[i].astype(BF16),
                             e_ffn_w2[i].astype(BF16), 1024, 256, f"even{i}_ffn")
        else:
            q, k, v = norm_proj(h, o_norm1[i], o_w_qkv[i].astype(BF16), (0, d, 2 * d, 3 * d),
                                (BF16, BF16, BF16), 512, f"odd{i}_qkv_proj")
            att = stick_breaking(q, k, v, batch, f"odd{i}_stick_breaking")
            h = proj_residual(h, o_w_out[i].astype(BF16), [att], 1024, f"odd{i}_out_proj")
            last = layer == depth - 1
            h = moe_residual(h, o_norm2[i], o_router[i], moe_w1, moe_w3, moe_w2, i,
                             final_norm if last else None, f"odd{i}_moe")
    if depth % 2 == 1:
        h = final_norm_call(h, final_norm, 1024, "final_norm")
    return h.reshape(batch, seq, d)
```

```python
import functools

import jax
import jax.numpy as jnp
from jax import lax
from jax.experimental import pallas as pl
from jax.experimental.pallas import tpu as pltpu

F32 = jnp.float32
BF16 = jnp.bfloat16

D_MODEL = 1024
EPS = 1e-6
CHUNK = 64
SGU_BLOCK = 128
SGU_GROUPS = 8
SGU_WIDTH = 512
SGU_GDIM = 64
GDN_HEADS = 4
GDN_DK = 128
GDN_DV = 128
GDN_CONV = 4
GDN_QK = 512
GDN_V = 512
CONV_CH = 1536
SB_HEADS = 16
SB_DIM = 64
D_FF = 2816
N_EXPERTS = 8
LANES = 128
VMEM_LIMIT = 48 * 1024 * 1024


def _params(sem):
    return pltpu.CompilerParams(dimension_semantics=sem, vmem_limit_bytes=VMEM_LIMIT)


def _rms(x, g):
    return x * lax.rsqrt(jnp.mean(x * x, axis=-1, keepdims=True) + EPS) * g


def _sigmoid(x):
    return 1.0 / (1.0 + jnp.exp(-x))


def _softplus(x):
    return jnp.maximum(x, 0.0) + jnp.log1p(jnp.exp(-jnp.abs(x)))


def _gelu(x):
    return 0.5 * x * (1.0 + jnp.tanh(0.7978845608028654 * (x + 0.044715 * (x * x * x))))


def _dot(a, b):
    return jnp.dot(a, b, preferred_element_type=F32)


def _dot_nt(a, b):
    return lax.dot_general(a, b, (((1,), (1,)), ((), ())), preferred_element_type=F32)


def _dot_tn(a, b):
    return lax.dot_general(a, b, (((0,), (0,)), ((), ())), preferred_element_type=F32)


def _norm_proj_kernel(h_ref, g_ref, w_ref, *o_refs, splits):
    xn = _rms(h_ref[...], g_ref[...]).astype(BF16)
    for o_ref, a, b in zip(o_refs, splits[:-1], splits[1:]):
        o_ref[...] = _dot(xn, w_ref[:, a:b]).astype(o_ref.dtype)


def norm_proj(h, g, w, splits, dtypes, tm, name):
    n, d = h.shape
    outs = [jax.ShapeDtypeStruct((n, b - a), dt) for a, b, dt in zip(splits[:-1], splits[1:], dtypes)]
    return pl.pallas_call(
        functools.partial(_norm_proj_kernel, splits=splits),
        grid=(n // tm,),
        in_specs=[pl.BlockSpec((tm, d), lambda i: (i, 0)),
                  pl.BlockSpec((1, d), lambda i: (0, 0)),
                  pl.BlockSpec(w.shape, lambda i: (0, 0))],
        out_specs=[pl.BlockSpec((tm, o.shape[1]), lambda i: (i, 0)) for o in outs],
        out_shape=outs,
        compiler_params=_params(("parallel",)),
        name=name,
    )(h, g.reshape(1, d), w)


def _proj_res_kernel(h_ref, w_ref, *refs, splits):
    x_refs, o_ref = refs[:-1], refs[-1]
    acc = h_ref[...]
    for x_ref, a, b in zip(x_refs, splits[:-1], splits[1:]):
        acc = acc + _dot(x_ref[...], w_ref[a:b, :])
    o_ref[...] = acc


def proj_residual(h, w, xs, tm, name):
    n, d = h.shape
    splits = [0]
    for x in xs:
        splits.append(splits[-1] + x.shape[1])
    return pl.pallas_call(
        functools.partial(_proj_res_kernel, splits=tuple(splits)),
        grid=(n // tm,),
        in_specs=[pl.BlockSpec((tm, d), lambda i: (i, 0)),
                  pl.BlockSpec(w.shape, lambda i: (0, 0))]
                 + [pl.BlockSpec((tm, x.shape[1]), lambda i: (i, 0)) for x in xs],
        out_specs=pl.BlockSpec((tm, d), lambda i: (i, 0)),
        out_shape=jax.ShapeDtypeStruct((n, d), F32),
        compiler_params=_params(("parallel",)),
        name=name,
    )(h, w, *xs)


def _ffn_kernel(h_ref, g_ref, w1_ref, w3_ref, w2_ref, o_ref, xn_ref, acc_ref):
    j = pl.program_id(1)

    @pl.when(j == 0)
    def _():
        xn_ref[...] = _rms(h_ref[...], g_ref[...]).astype(BF16)
        acc_ref[...] = jnp.zeros_like(acc_ref)

    xn = xn_ref[...]
    a = _dot(xn, w1_ref[...])
    b = _dot(xn, w3_ref[...])
    hidden = (a * _sigmoid(a) * b).astype(BF16)
    acc_ref[...] += _dot(hidden, w2_ref[...])

    @pl.when(j == pl.num_programs(1) - 1)
    def _():
        o_ref[...] = h_ref[...] + acc_ref[...]


def ffn_residual(h, g, w1, w3, w2, tm, tf, name):
    n, d = h.shape
    f = w1.shape[1]
    return pl.pallas_call(
        _ffn_kernel,
        grid=(n // tm, f // tf),
        in_specs=[pl.BlockSpec((tm, d), lambda i, j: (i, 0)),
                  pl.BlockSpec((1, d), lambda i, j: (0, 0)),
                  pl.BlockSpec((d, tf), lambda i, j: (0, j)),
                  pl.BlockSpec((d, tf), lambda i, j: (0, j)),
                  pl.BlockSpec((tf, d), lambda i, j: (j, 0))],
        out_specs=pl.BlockSpec((tm, d), lambda i, j: (i, 0)),
        out_shape=jax.ShapeDtypeStruct((n, d), F32),
        scratch_shapes=[pltpu.VMEM((tm, d), BF16), pltpu.VMEM((tm, d), F32)],
        compiler_params=_params(("parallel", "arbitrary")),
        name=name,
    )(h, g.reshape(1, d), w1, w3, w2)


MOE_TM = 512
ROUTER_TM = 256
ROW_START_UNROLL = 4
ROW_WAIT_UNROLL = 16
META_E1, META_E2, META_R1, META_R2, META_G1, META_G2 = range(6)


def _router_kernel(h_ref, g_ref, wr_ref, meta_ref, cnt_ref, run_ref):
    @pl.when(pl.program_id(0) == 0)
    def _():
        run_ref[...] = jnp.zeros_like(run_ref)

    xn = _rms(h_ref[...], g_ref[...])
    logits = jnp.dot(xn, wr_ref[...], preferred_element_type=F32,
                     precision=lax.Precision.HIGHEST)
    tm = logits.shape[0]
    lane = lax.broadcasted_iota(jnp.int32, logits.shape, 1)
    neg = jnp.float32(-jnp.inf)
    l1 = jnp.where(lane < N_EXPERTS, logits, neg)
    m1 = jnp.max(l1, axis=-1, keepdims=True)
    i1 = jnp.min(jnp.where(l1 == m1, lane, LANES), axis=-1, keepdims=True)
    l2 = jnp.where(lane == i1, neg, l1)
    m2 = jnp.max(l2, axis=-1, keepdims=True)
    i2 = jnp.min(jnp.where(l2 == m2, lane, LANES), axis=-1, keepdims=True)
    e2 = jnp.exp(m2 - m1)
    g1 = 1.0 / (1.0 + e2)
    g2 = e2 / (1.0 + e2)

    chosen = (lane == i1) | (lane == i2)
    onehot = jnp.where(chosen, 1.0, 0.0)
    rr = lax.broadcasted_iota(jnp.int32, (tm, tm), 0)
    cc = lax.broadcasted_iota(jnp.int32, (tm, tm), 1)
    before = jnp.where(cc < rr, 1.0, 0.0).astype(BF16)
    prefix = _dot(before, onehot.astype(BF16)) + run_ref[0:1, :]
    r1 = jnp.sum(jnp.where(lane == i1, prefix, 0.0), axis=-1, keepdims=True)
    r2 = jnp.sum(jnp.where(lane == i2, prefix, 0.0), axis=-1, keepdims=True)
    run_ref[...] = run_ref[...] + jnp.sum(onehot, axis=0, keepdims=True)
    cnt_ref[...] = run_ref[...]

    meta = jnp.zeros(logits.shape, F32)
    for idx, val in ((META_E1, i1.astype(F32)), (META_E2, i2.astype(F32)), (META_R1, r1), (META_R2, r2),
                     (META_G1, g1), (META_G2, g2)):
        meta = jnp.where(lane == idx, val, meta)
    meta_ref[...] = meta


def router(h, g, wr_pad, name):
    n, d = h.shape
    tm = ROUTER_TM
    return pl.pallas_call(
        _router_kernel,
        grid=(n // tm,),
        in_specs=[pl.BlockSpec((tm, d), lambda i: (i, 0)),
                  pl.BlockSpec((1, d), lambda i: (0, 0)),
                  pl.BlockSpec((d, LANES), lambda i: (0, 0))],
        out_specs=[pl.BlockSpec((tm, LANES), lambda i: (i, 0)),
                   pl.BlockSpec((8, LANES), lambda i: (0, 0))],
        out_shape=[jax.ShapeDtypeStruct((n, LANES), F32), jax.ShapeDtypeStruct((8, LANES), F32)],
        scratch_shapes=[pltpu.VMEM((8, LANES), F32)],
        compiler_params=_params(("arbitrary",)),
        name=name,
    )(h, g.reshape(1, d), wr_pad)


def _dispatch_kernel(pos1_ref, pos2_ref, h_ref, g_ref, xs_init_hbm, xs_hbm, xn_ref, sem, *, tm):
    del xs_init_hbm
    xn_ref[...] = _rms(h_ref[...], g_ref[...])

    def copies(r):
        src = xn_ref.at[pl.ds(r, 1)]
        return (pltpu.make_async_copy(src, xs_hbm.at[pl.ds(pos1_ref[r], 1)], sem.at[0]),
                pltpu.make_async_copy(src, xs_hbm.at[pl.ds(pos2_ref[r], 1)], sem.at[1]))

    def start(r, carry):
        for cp in copies(r):
            cp.start()
        return carry

    def wait(r, carry):
        for cp in copies(r):
            cp.wait()
        return carry

    lax.fori_loop(0, tm, start, 0, unroll=ROW_START_UNROLL)
    lax.fori_loop(0, tm, wait, 0, unroll=ROW_WAIT_UNROLL)


def dispatch(h, g, pos1, pos2, n_rows, tm, name):
    n, d = h.shape
    return pl.pallas_call(
        functools.partial(_dispatch_kernel, tm=tm),
        grid=(n // tm,),
        in_specs=[pl.BlockSpec((tm,), lambda i: (i,), memory_space=pltpu.SMEM),
                  pl.BlockSpec((tm,), lambda i: (i,), memory_space=pltpu.SMEM),
                  pl.BlockSpec((tm, d), lambda i: (i, 0)),
                  pl.BlockSpec((1, d), lambda i: (0, 0)),
                  pl.BlockSpec(memory_space=pl.ANY)],
        out_specs=pl.BlockSpec(memory_space=pl.ANY),
        out_shape=jax.ShapeDtypeStruct((n_rows, d), F32),
        scratch_shapes=[pltpu.VMEM((tm, d), F32), pltpu.SemaphoreType.DMA((2,))],
        input_output_aliases={4: 0},
        compiler_params=_params(("arbitrary",)),
        name=name,
    )(pos1, pos2, h, g.reshape(1, d), jnp.zeros((n_rows, d), F32))


def _experts_kernel(te_ref, nt_ref, xs_ref, w1_ref, w3_ref, w2_ref, ys_ref, xn_ref, acc_ref):
    del te_ref
    i = pl.program_id(0)
    j = pl.program_id(1)

    @pl.when(i < nt_ref[0])
    def _():
        @pl.when(j == 0)
        def _():
            xn_ref[...] = xs_ref[...].astype(BF16)
            acc_ref[...] = jnp.zeros_like(acc_ref)

        xn = xn_ref[...]
        a = _dot(xn, w1_ref[...])
        b = _dot(xn, w3_ref[...])
        hidden = (a * _sigmoid(a) * b).astype(BF16)
        acc_ref[...] += _dot(hidden, w2_ref[...])

        @pl.when(j == pl.num_programs(1) - 1)
        def _():
            ys_ref[...] = acc_ref[...]

    @pl.when((i >= nt_ref[0]) & (j == pl.num_programs(1) - 1))
    def _():
        ys_ref[...] = jnp.zeros_like(ys_ref)


def experts(xs, tile_expert, n_tiles, w1, w3, w2, layer, tf, name):
    n_rows, d = xs.shape
    f = w1.shape[3]
    tm = MOE_TM
    nj = f // tf

    def used(i, nt):
        return jnp.maximum(jnp.minimum(i, nt[0] - 1), 0)

    def row_map(i, j, te, nt):
        return (used(i, nt), 0)

    def up_map(i, j, te, nt):
        return (layer, te[used(i, nt)], 0, jnp.where(i < nt[0], j, nj - 1))

    def down_map(i, j, te, nt):
        return (layer, te[used(i, nt)], jnp.where(i < nt[0], j, nj - 1), 0)

    return pl.pallas_call(
        _experts_kernel,
        grid_spec=pltpu.PrefetchScalarGridSpec(
            num_scalar_prefetch=2,
            grid=(n_rows // tm, nj),
            in_specs=[pl.BlockSpec((tm, d), row_map),
                      pl.BlockSpec((None, None, d, tf), up_map),
                      pl.BlockSpec((None, None, d, tf), up_map),
                      pl.BlockSpec((None, None, tf, d), down_map)],
            out_specs=pl.BlockSpec((tm, d), lambda i, j, te, nt: (i, 0)),
            scratch_shapes=[pltpu.VMEM((tm, d), BF16), pltpu.VMEM((tm, d), F32)]),
        out_shape=jax.ShapeDtypeStruct((n_rows, d), F32),
        compiler_params=_params(("arbitrary", "arbitrary")),
        name=name,
    )(tile_expert, n_tiles, xs, w1, w3, w2)


def _combine_kernel(pos1_ref, pos2_ref, meta_ref, h_ref, gf_ref, ys_hbm, o_ref, buf_ref, sem, *, tm, final):
    def copies(r):
        return (pltpu.make_async_copy(ys_hbm.at[pl.ds(pos1_ref[r], 1)], buf_ref.at[0, pl.ds(r, 1)], sem.at[0]),
                pltpu.make_async_copy(ys_hbm.at[pl.ds(pos2_ref[r], 1)], buf_ref.at[1, pl.ds(r, 1)], sem.at[1]))

    def start(r, carry):
        for cp in copies(r):
            cp.start()
        return carry

    def wait(r, carry):
        for cp in copies(r):
            cp.wait()
        return carry

    lax.fori_loop(0, tm, start, 0, unroll=ROW_START_UNROLL)
    lax.fori_loop(0, tm, wait, 0, unroll=ROW_WAIT_UNROLL)
    meta = meta_ref[...]
    out = (h_ref[...] + meta[:, META_G1:META_G1 + 1] * buf_ref[0]
           + meta[:, META_G2:META_G2 + 1] * buf_ref[1])
    if final:
        out = _rms(out, gf_ref[...])
    o_ref[...] = out


def combine(h, meta, pos1, pos2, ys, final_g, tm, name):
    n, d = h.shape
    final = final_g is not None
    gf = (final_g if final else jnp.ones((d,), F32)).reshape(1, d)
    return pl.pallas_call(
        functools.partial(_combine_kernel, tm=tm, final=final),
        grid=(n // tm,),
        in_specs=[pl.BlockSpec((tm,), lambda i: (i,), memory_space=pltpu.SMEM),
                  pl.BlockSpec((tm,), lambda i: (i,), memory_space=pltpu.SMEM),
                  pl.BlockSpec((tm, LANES), lambda i: (i, 0)),
                  pl.BlockSpec((tm, d), lambda i: (i, 0)),
                  pl.BlockSpec((1, d), lambda i: (0, 0)),
                  pl.BlockSpec(memory_space=pl.ANY)],
        out_specs=pl.BlockSpec((tm, d), lambda i: (i, 0)),
        out_shape=jax.ShapeDtypeStruct((n, d), F32),
        scratch_shapes=[pltpu.VMEM((2, tm, d), F32), pltpu.SemaphoreType.DMA((2,))],
        compiler_params=_params(("arbitrary",)),
        name=name,
    )(pos1, pos2, meta, h, gf, ys)


def moe_residual(h, g, w_router, w1, w3, w2, layer, final_g, name):
    n, d = h.shape
    wr = jnp.concatenate([w_router, jnp.zeros((d, LANES - N_EXPERTS), F32)], axis=1)
    meta, cnt = router(h, g, wr, f"{name}_router")
    cnt = cnt[0, :N_EXPERTS].astype(jnp.int32)
    n_tile_e = (cnt + MOE_TM - 1) // MOE_TM
    tile_end = jnp.cumsum(n_tile_e)
    offset = (tile_end - n_tile_e) * MOE_TM
    e1 = meta[:, META_E1].astype(jnp.int32)
    e2 = meta[:, META_E2].astype(jnp.int32)
    pos1 = offset[e1] + meta[:, META_R1].astype(jnp.int32)
    pos2 = offset[e2] + meta[:, META_R2].astype(jnp.int32)
    max_tiles = (2 * n) // MOE_TM + N_EXPERTS
    tile_ids = jnp.arange(max_tiles, dtype=jnp.int32)
    tile_expert = jnp.minimum(jnp.sum(tile_ids[:, None] >= tile_end[None, :], axis=1), N_EXPERTS - 1)
    n_tiles = tile_end[-1:].astype(jnp.int32)
    xs = dispatch(h, g, pos1, pos2, max_tiles * MOE_TM, 1024, f"{name}_dispatch")
    ys = experts(xs, tile_expert.astype(jnp.int32), n_tiles, w1, w3, w2, layer, 1408, f"{name}_experts")
    return combine(h, meta, pos1, pos2, ys, final_g, 512, f"{name}_combine")


def _final_norm_kernel(h_ref, g_ref, o_ref):
    o_ref[...] = _rms(h_ref[...], g_ref[...])


def final_norm_call(h, g, tm, name):
    n, d = h.shape
    return pl.pallas_call(
        _final_norm_kernel,
        grid=(n // tm,),
        in_specs=[pl.BlockSpec((tm, d), lambda i: (i, 0)),
                  pl.BlockSpec((1, d), lambda i: (0, 0))],
        out_specs=pl.BlockSpec((tm, d), lambda i: (i, 0)),
        out_shape=jax.ShapeDtypeStruct((n, d), F32),
        compiler_params=_params(("parallel",)),
        name=name,
    )(h, g.reshape(1, d))


def _sgu_kernel(uv_ref, lng_ref, lnb_ref, wcat_ref, bias_ref, o_ref):
    uv = uv_ref[...]
    u = _gelu(uv[:, :SGU_WIDTH])
    v = _gelu(uv[:, SGU_WIDTH:])
    mu = jnp.mean(v, axis=-1, keepdims=True)
    vc = v - mu
    var = jnp.mean(vc * vc, axis=-1, keepdims=True)
    v = (vc * lax.rsqrt(var + EPS) * lng_ref[...] + lnb_ref[...]).astype(BF16)

    kdim = SGU_GROUPS * SGU_BLOCK
    row_g = lax.broadcasted_iota(jnp.int32, (kdim, SGU_WIDTH), 0) // SGU_BLOCK
    col_g = lax.broadcasted_iota(jnp.int32, (kdim, SGU_WIDTH), 1) // SGU_GDIM
    vbd = jnp.where(row_g == col_g, jnp.concatenate([v] * SGU_GROUPS, axis=0), jnp.zeros((), BF16))

    t_chunk = lax.broadcasted_iota(jnp.int32, (SGU_BLOCK, kdim), 0) // CHUNK
    s_chunk = (lax.broadcasted_iota(jnp.int32, (SGU_BLOCK, kdim), 1) % SGU_BLOCK) // CHUNK
    w = jnp.where(s_chunk <= t_chunk, wcat_ref[...], 0.0).astype(BF16)
    mixed = _dot(w, vbd) + bias_ref[...]
    o_ref[...] = (u * mixed).astype(o_ref.dtype)


def sgu(uv, ln_g, ln_b, w_s, b_s, name):
    n = uv.shape[0]
    wcat = jnp.transpose(w_s, (1, 0, 2)).reshape(SGU_BLOCK, SGU_GROUPS * SGU_BLOCK)
    bias = jnp.repeat(b_s.T, SGU_GDIM, axis=1)
    return pl.pallas_call(
        _sgu_kernel,
        grid=(n // SGU_BLOCK,),
        in_specs=[pl.BlockSpec((SGU_BLOCK, 2 * SGU_WIDTH), lambda i: (i, 0)),
                  pl.BlockSpec((1, SGU_WIDTH), lambda i: (0, 0)),
                  pl.BlockSpec((1, SGU_WIDTH), lambda i: (0, 0)),
                  pl.BlockSpec(wcat.shape, lambda i: (0, 0)),
                  pl.BlockSpec(bias.shape, lambda i: (0, 0))],
        out_specs=pl.BlockSpec((SGU_BLOCK, SGU_WIDTH), lambda i: (i, 0)),
        out_shape=jax.ShapeDtypeStruct((n, SGU_WIDTH), BF16),
        compiler_params=_params(("parallel",)),
        name=name,
    )(uv, ln_g.reshape(1, -1), ln_b.reshape(1, -1), wcat, bias)


GDN_T = 256
HALO = 8


def _bdot(a, b):
    return lax.dot_general(a, b, (((2,), (1,)), ((0,), (0,))), preferred_element_type=F32)


def _bdot_nt(a, b):
    return lax.dot_general(a, b, (((2,), (2,)), ((0,), (0,))), preferred_element_type=F32)


def _bdot_tn(a, b):
    return lax.dot_general(a, b, (((1,), (1,)), ((0,), (0,))), preferred_element_type=F32)


def _split3(x):
    hi = x.astype(BF16)
    r1 = x - hi.astype(F32)
    mid = r1.astype(BF16)
    lo = (r1 - mid.astype(F32)).astype(BF16)
    return hi, mid, lo


def _gdn_kernel(qkv_ref, gate_ref, ba_ref, convw_ref, arow_ref, onorm_ref, o_ref, state_ref, xbuf_ref):
    tb = pl.program_id(1)
    nc = GDN_T // CHUNK
    nu = nc * GDN_HEADS

    @pl.when(tb == 0)
    def _():
        state_ref[...] = jnp.zeros_like(state_ref)
        xbuf_ref[0:HALO, :] = jnp.zeros((HALO, CONV_CH), F32)

    @pl.when(tb != 0)
    def _():
        xbuf_ref[0:HALO, :] = xbuf_ref[GDN_T:GDN_T + HALO, :]

    xbuf_ref[HALO:HALO + GDN_T, :] = qkv_ref[...]
    cw = convw_ref[...]
    y = cw[0:1, :] * xbuf_ref[pl.ds(HALO - 3, GDN_T), :]
    for i in range(1, GDN_CONV):
        y = y + cw[i:i + 1, :] * xbuf_ref[pl.ds(HALO - 3 + i, GDN_T), :]
    qkv = y * _sigmoid(y)

    arow = arow_ref[...]
    ba = ba_ref[...]
    beta_full = _sigmoid(ba)
    g_full = arow[0:1, :] * _softplus(ba + arow[1:2, :])

    ii = lax.broadcasted_iota(jnp.int32, (CHUNK, CHUNK), 0)
    jj = lax.broadcasted_iota(jnp.int32, (CHUNK, CHUNK), 1)
    causal = ii >= jj
    strict = ii > jj

    tril = jnp.broadcast_to(jnp.where(causal, 1.0, 0.0).astype(BF16), (nc, CHUNK, CHUNK))
    gc_full = sum(_bdot(tril, part) for part in _split3(g_full.reshape(nc, CHUNK, LANES)))

    def units(fn):
        return jnp.stack([fn(c, h) for c in range(nc) for h in range(GDN_HEADS)], axis=0)

    def rows(c):
        return slice(c * CHUNK, (c + 1) * CHUNK)

    def l2n(x):
        return x * lax.rsqrt(jnp.sum(x * x, axis=-1, keepdims=True) + EPS)

    q = l2n(units(lambda c, h: qkv[rows(c), h * GDN_DK:(h + 1) * GDN_DK])) * (GDN_DK ** -0.5)
    k = l2n(units(lambda c, h: qkv[rows(c), GDN_QK + h * GDN_DK:GDN_QK + (h + 1) * GDN_DK]))
    v = units(lambda c, h: qkv[rows(c), 2 * GDN_QK + h * GDN_DV:2 * GDN_QK + (h + 1) * GDN_DV])
    beta = units(lambda c, h: beta_full[rows(c), h:h + 1])
    gc = units(lambda c, h: gc_full[c, :, 4 + h:5 + h])

    pick0 = jnp.broadcast_to(jnp.where(jj == 0, 1.0, 0.0).astype(BF16), (nu, CHUNK, CHUNK))
    gc_b = jnp.broadcast_to(gc, (nu, CHUNK, CHUNK))
    gc_row = sum(_bdot_nt(pick0, part) for part in _split3(gc_b))
    gamma = jnp.where(causal, jnp.exp(jnp.where(causal, gc - gc_row, 0.0)), 0.0)
    g_last = gc[:, CHUNK - 1:CHUNK, :]
    egc = jnp.exp(gc)

    kb = k * beta
    kbf = k.astype(BF16)
    lmat = jnp.where(strict, _bdot_nt(kb.astype(BF16), kbf) * gamma, 0.0)
    attn = (_bdot_nt(q.astype(BF16), kbf) * gamma).astype(BF16)
    x = jnp.concatenate([v * beta, kb * egc], axis=2)
    npow = -lmat
    for level in range(6):
        nb = npow.astype(BF16)
        x = x + _bdot(nb, x.astype(BF16))
        if level < 5:
            npow = _bdot(nb, nb)
    u = x[:, :, :GDN_DV].astype(BF16)
    w = x[:, :, GDN_DV:].astype(BF16)
    k_dec = (k * jnp.exp(g_last - gc)).astype(BF16)
    q_eff = (q * egc - _bdot(attn, w)).astype(BF16)
    o_loc = _bdot(attn, u)
    s_mix = _bdot_tn(k_dec, w).astype(BF16)
    s_add = _bdot_tn(k_dec, u)
    decay = jnp.exp(g_last)

    onorm = onorm_ref[...]
    gate = gate_ref[...]
    state = state_ref[...]
    for c in range(nc):
        sl = slice(c * GDN_HEADS, (c + 1) * GDN_HEADS)
        sb = state.astype(BF16)
        o = _bdot(q_eff[sl], sb) + o_loc[sl]
        state = decay[sl] * state - _bdot(s_mix[sl], sb) + s_add[sl]
        o = o * lax.rsqrt(jnp.mean(o * o, axis=-1, keepdims=True) + EPS) * onorm
        for h in range(GDN_HEADS):
            gt = gate[rows(c), h * GDN_DV:(h + 1) * GDN_DV]
            o_ref[rows(c), h * GDN_DV:(h + 1) * GDN_DV] = (o[h] * (gt * _sigmoid(gt))).astype(o_ref.dtype)
    state_ref[...] = state


def gdn(qkv, gate, ba, conv_w, a_log, dt_bias, o_norm, batch, name):
    n = qkv.shape[0]
    s = n // batch
    nt = s // GDN_T
    arow = jnp.zeros((8, LANES), F32).at[0, 4:8].set(-jnp.exp(a_log)).at[1, 4:8].set(dt_bias)
    return pl.pallas_call(
        _gdn_kernel,
        grid=(batch, nt),
        in_specs=[pl.BlockSpec((GDN_T, CONV_CH), lambda b, t: (b * nt + t, 0)),
                  pl.BlockSpec((GDN_T, GDN_V), lambda b, t: (b * nt + t, 0)),
                  pl.BlockSpec((GDN_T, LANES), lambda b, t: (b * nt + t, 0)),
                  pl.BlockSpec((GDN_CONV, CONV_CH), lambda b, t: (0, 0)),
                  pl.BlockSpec((8, LANES), lambda b, t: (0, 0)),
                  pl.BlockSpec((1, GDN_DV), lambda b, t: (0, 0))],
        out_specs=pl.BlockSpec((GDN_T, GDN_V), lambda b, t: (b * nt + t, 0)),
        out_shape=jax.ShapeDtypeStruct((n, GDN_V), BF16),
        scratch_shapes=[pltpu.VMEM((GDN_HEADS, GDN_DK, GDN_DV), F32),
                        pltpu.VMEM((HALO + GDN_T, CONV_CH), F32)],
        compiler_params=_params(("parallel", "arbitrary")),
        name=name,
    )(qkv, gate, ba, conv_w, arow, o_norm.reshape(1, -1))


SB_T = 256
SB_PAIRS = 4
SB_DEAD = 120.0


def _sb_kernel(q_ref, k_ref, v_ref, fk_ref, o_ref, qh_ref, z_ref, sp_ref, acc_ref, cs_ref):
    qi = pl.program_id(2)
    lane = lax.broadcasted_iota(jnp.int32, (1, LANES), 1)
    heads = [(p, hh) for p in range(SB_PAIRS) for hh in range(2)]

    for idx, (p, hh) in enumerate(heads):
        q = q_ref[:, p * LANES:(p + 1) * LANES]
        in_head = (lane >= hh * SB_DIM) & (lane < (hh + 1) * SB_DIM)
        qh_ref[idx] = jnp.where(in_head, q, jnp.zeros((), BF16)) * jnp.asarray(SB_DIM ** -0.5, BF16)
    acc_ref[...] = jnp.zeros_like(acc_ref)
    cs_ref[...] = jnp.zeros_like(cs_ref)

    def earlier():
        row = lax.broadcasted_iota(jnp.int32, (SB_T, SB_T), 0)
        col = lax.broadcasted_iota(jnp.int32, (SB_T, SB_T), 1)
        return col < row

    def stage_a(kb, slot, diagonal=False):
        start = pl.multiple_of(kb * SB_T, SB_T)
        for idx, (p, hh) in enumerate(heads):
            kblk = k_ref[pl.ds(start, SB_T), p * LANES:(p + 1) * LANES]
            z = _dot_nt(qh_ref[idx], kblk)
            sp = jnp.maximum(z, 0.0) + jnp.log(1.0 + jnp.exp(-jnp.abs(z)))
            if diagonal:
                sp = jnp.where(earlier(), sp, 0.0)
            z_ref[slot, idx] = z
            sp_ref[slot, idx] = sp.astype(BF16)

    def stage_b(kb, slot, diagonal=False):
        start = pl.multiple_of(kb * SB_T, SB_T)
        for idx, (p, hh) in enumerate(heads):
            vblk = v_ref[pl.ds(start, SB_T), p * LANES:(p + 1) * LANES]
            ssum = _dot(sp_ref[slot, idx], fk_ref[...])
            att = jnp.exp(jnp.minimum(z_ref[slot, idx] - ssum, 0.0) - cs_ref[idx])
            if diagonal:
                att = jnp.where(earlier(), att, 0.0)
            acc_ref[idx] += _dot(att.astype(BF16), vblk)
            cs_ref[idx] += ssum[:, 0:1]

    def live():
        return (jnp.min(cs_ref[...]) < SB_DEAD).astype(jnp.int32)

    @pl.when(qi == 0)
    def _():
        stage_a(0, 0, True)
        stage_b(0, 0, True)

    @pl.when(qi > 0)
    def _():
        stage_a(qi, 0, True)
        stage_a(qi - 1, 1)
        stage_b(qi, 0, True)
        stage_b(qi - 1, 1)

        def more(carry):
            kb, go = carry
            return (kb >= 0) & (go > 0)

        def one_tile(carry):
            kb, _ = carry
            stage_a(kb, 0)
            stage_b(kb, 0)
            return kb - 1, live()

        lax.while_loop(more, one_tile, (qi - 2, live()))

    for p in range(SB_PAIRS):
        o_ref[:, p * LANES:(p + 1) * LANES] = jnp.where(
            lane < SB_DIM, acc_ref[2 * p], acc_ref[2 * p + 1]).astype(o_ref.dtype)


def stick_breaking(q, k, v, batch, name):
    n, d = q.shape
    s = n // batch
    width = SB_PAIRS * LANES
    nh = 2 * SB_PAIRS
    q3, k3, v3 = (t.reshape(batch, s, d) for t in (q, k, v))
    from_key = jnp.tril(jnp.ones((SB_T, SB_T), BF16))
    out = pl.pallas_call(
        _sb_kernel,
        grid=(batch, d // width, s // SB_T),
        in_specs=[pl.BlockSpec((None, SB_T, width), lambda b, p, i: (b, i, p)),
                  pl.BlockSpec((None, s, width), lambda b, p, i: (b, 0, p)),
                  pl.BlockSpec((None, s, width), lambda b, p, i: (b, 0, p)),
                  pl.BlockSpec((SB_T, SB_T), lambda b, p, i: (0, 0))],
        out_specs=pl.BlockSpec((None, SB_T, width), lambda b, p, i: (b, i, p)),
        out_shape=jax.ShapeDtypeStruct((batch, s, d), BF16),
        scratch_shapes=[pltpu.VMEM((nh, SB_T, LANES), BF16),
                        pltpu.VMEM((2, nh, SB_T, SB_T), F32),
                        pltpu.VMEM((2, nh, SB_T, SB_T), BF16),
                        pltpu.VMEM((nh, SB_T, LANES), F32),
                        pltpu.VMEM((nh, SB_T, 1), F32)],
        compiler_params=_params(("parallel", "parallel", "arbitrary")),
        name=name,
    )(q3, k3, v3, from_key)
    return out.reshape(n, d)


def kernel(x, e_norm1, e_w_in, e_sgu_ln_g, e_sgu_ln_b, e_sgu_w, e_sgu_b, e_conv_w, e_a_log, e_dt_bias, e_o_norm, e_w_out, e_norm2, e_ffn_w1, e_ffn_w3, e_ffn_w2, o_norm1, o_w_qkv, o_w_out, o_norm2, o_router, o_moe_w1, o_moe_w3, o_moe_w2, final_norm):
    batch, seq, d = x.shape
    n = batch * seq
    h = x.reshape(n, d)
    n_main = 2 * SGU_WIDTH + CONV_CH + GDN_V
    depth = e_norm1.shape[0] + o_norm1.shape[0]
    moe_w1, moe_w3, moe_w2 = (w.astype(BF16) for w in (o_moe_w1, o_moe_w3, o_moe_w2))
    for layer in range(depth):
        i = layer // 2
        if layer % 2 == 0:
            w_in = e_w_in[i]
            w_in = jnp.concatenate(
                [w_in, jnp.zeros((d, LANES - (w_in.shape[1] - n_main)), F32)], axis=1).astype(BF16)
            splits = (0, 2 * SGU_WIDTH, 2 * SGU_WIDTH + CONV_CH, n_main, n_main + LANES)
            uv, qkv, gate, ba = norm_proj(h, e_norm1[i], w_in, splits, (F32, F32, F32, F32), 512,
                                          f"even{i}_in_proj")
            y_a = sgu(uv, e_sgu_ln_g[i], e_sgu_ln_b[i], e_sgu_w[i], e_sgu_b[i], f"even{i}_sgu")
            y_b = gdn(qkv, gate, ba, e_conv_w[i], e_a_log[i], e_dt_bias[i], e_o_norm[i], batch,
                      f"even{i}_gdn")
            h = proj_residual(h, e_w_out[i].astype(BF16), [y_a, y_b], 1024, f"even{i}_out_proj")
            h = ffn_residual(h, e_norm2[i], e_ffn_w1[i].astype(BF16), e_ffn_w3[i].astype(BF16),
                             e_ffn_w2[i].astype(BF16), 1024, 256, f"even{i}_ffn")
        else:
            q, k, v = norm_proj(h, o_norm1[i], o_w_qkv[i].astype(BF16), (0, d, 2 * d, 3 * d),
                                (BF16, BF16, BF16), 512, f"odd{i}_qkv_proj")
            att = stick_breaking(q, k, v, batch, f"odd{i}_stick_breaking")
            h = proj_residual(h, o_w_out[i].astype(BF16), [att], 1024, f"odd{i}_out_proj")
            last = layer == depth - 1
            h = moe_residual(h, o_norm2[i], o_router[i], moe_w1, moe_w3, moe_w2, i,
                             final_norm if last else None, f"odd{i}_moe")
    if depth % 2 == 1:
        h = final_norm_call(h, final_norm, 1024, "final_norm")
    return h.reshape(batch, seq, d)
```
